```python
import jax, jax.numpy as jnp
from jax import lax
import numpy as np

D_MODEL = 2048
BATCH = 16
SEQ = 256
DEPTH = 4
DEC_BATCH = 2
DEC_SEQ = 1024
PAST_LEN = 512

GRID_W = 64
N_MIXERS = 3
MIX_MLSTM = 0
MIX_ATTN = 1
MIX_SSD = 2
N_MLSTM_LAYERS = (DEPTH + 2) // 3
N_ATTN_LAYERS = (DEPTH + 1) // 3
N_SSD_LAYERS = DEPTH // 3
ALPHA = (2 * DEPTH) ** 0.25
BETA = (8 * DEPTH) ** -0.25
LN_EPS = 1e-6
N_MOD = 9
D_FF = 5632
ML_HEADS = 8
ML_DK = 128
ML_DV = 256
ML_CHUNK = 128
ML_IN = 2 * ML_HEADS * ML_DK + 2 * ML_HEADS * ML_DV + 4 * ML_HEADS
ATTN_HEADS = 16
ATTN_KV_HEADS = 4
ATTN_HD = 128
ATTN_GROUP = ATTN_HEADS // ATTN_KV_HEADS
ATTN_IN = (ATTN_HEADS + 2 * ATTN_KV_HEADS) * ATTN_HD
Q_BLOCK = 128
ROPE_THETA = 10000.0
ROPE_PAIRS = ATTN_HD // 4
SSD_DI = 2 * D_MODEL
SSD_HD = 64
SSD_HEADS = SSD_DI // SSD_HD
SSD_GROUPS = 8
SSD_HPG = SSD_HEADS // SSD_GROUPS
SSD_N = 128
SSD_CONV_W = 5
SSD_CONV_CH = SSD_DI + 2 * SSD_GROUPS * SSD_N
SSD_IN = SSD_DI + SSD_CONV_CH + 2 * SSD_HEADS
SSD_CHUNK = 128

kernel_name = 'hybrid_mlstm_gqa_ssd_macaron_diffusion_step'


def layer_norm(x, g, b):
    xf = x.astype(jnp.float32)
    mu = xf.mean(-1, keepdims=True)
    var = jnp.square(xf - mu).mean(-1, keepdims=True)
    return ((xf - mu) * lax.rsqrt(var + LN_EPS) * g + b).astype(x.dtype)


def rms_norm(x, g):
    xf = x.astype(jnp.float32)
    return (xf * lax.rsqrt(jnp.mean(xf * xf, -1, keepdims=True) + LN_EPS) * g).astype(x.dtype)


def post_norm(x, y, g, b):
    return layer_norm(ALPHA * x + y, g, b)


def modulate(x, shift, scale):
    return x * (1 + scale) + shift


def adaln(cvec, w, b):
    return jnp.split(jax.nn.silu(cvec) @ w + b, N_MOD, axis=-1)


def swiglu(h, w_in, w_out):
    gate, up = jnp.split(h @ w_in, 2, axis=-1)
    return (jax.nn.silu(gate) * up) @ w_out


def ffn_step(x, shift, scale, gate, w_in, w_out, g, b):
    return post_norm(x, 0.5 * gate * swiglu(modulate(x, shift, scale), w_in, w_out), g, b)


def flip(a):
    return jnp.flip(a, axis=1)


def to_chunks(a, L):
    B, T = a.shape[:2]
    return jnp.moveaxis(a.reshape((B, T // L, L) + a.shape[2:]), 1, 0)


def from_chunks(a):
    nc, B, L = a.shape[:3]
    return jnp.moveaxis(a, 0, 1).reshape((B, nc * L) + a.shape[3:])


def mlstm_scan(q, k, v, logi, logf, C0, n0, m0):
    L = ML_CHUNK
    tril = jnp.tril(jnp.ones((L, L), bool))

    def step(carry, inp):
        C, n, m = carry
        qc, kc, vc, li, lf = inp
        b = jnp.cumsum(lf, axis=1)
        g = b[:, -1]
        d = b[:, :, None, :] - b[:, None, :, :] + li[:, None, :, :]
        d = jnp.where(tril[None, :, :, None], d, -jnp.inf)
        inter = b + m[:, None, :]
        m_t = jnp.maximum(inter, d.max(axis=2))
        w = jnp.exp(d - m_t[:, :, None, :])
        s = jnp.einsum('bthd,bshd->btsh', qc, kc) * w
        sc = jnp.exp(inter - m_t)
        num = sc[..., None] * jnp.einsum('bthd,bhde->bthe', qc, C) + jnp.einsum('btsh,bshe->bthe', s, vc)
        den = sc * jnp.einsum('bthd,bhd->bth', qc, n) + s.sum(axis=2)
        h = num / jnp.maximum(jnp.abs(den), jnp.exp(-m_t))[..., None]
        ds = g[:, None, :] - b + li
        m_new = jnp.maximum(g + m, ds.max(axis=1))
        wk = jnp.exp(ds - m_new[:, None, :])[..., None] * kc
        decay = jnp.exp(g + m - m_new)
        C_new = decay[..., None, None] * C + jnp.einsum('bshd,bshe->bhde', wk, vc)
        n_new = decay[..., None] * n + wk.sum(axis=1)
        return (C_new, n_new, m_new), h

    carry0 = (C0.astype(jnp.float32), n0.astype(jnp.float32), m0.astype(jnp.float32))
    xs = tuple(to_chunks(a, L) for a in (q, k, v, logi, logf))
    (C, n, m), h = lax.scan(step, carry0, xs)
    return from_chunks(h), (C, n, m)


def mlstm_mixer(h, w_in, b_gate, norm_g, w_out, state_f, state_b):
    B, T, _ = h.shape
    qk, vd = ML_HEADS * ML_DK, ML_HEADS * ML_DV
    q, k, v, o, gates = jnp.split(h @ w_in, [qk, 2 * qk, 2 * qk + vd, 2 * qk + 2 * vd], axis=-1)
    q = q.reshape(B, T, ML_HEADS, ML_DK) * (ML_DK ** -0.5)
    k = k.reshape(B, T, ML_HEADS, ML_DK)
    v = v.reshape(B, T, ML_HEADS, ML_DV)
    gates = (gates + b_gate).astype(jnp.float32).reshape(B, T, 2, 2, ML_HEADS)
    logi = gates[:, :, 0]
    logf = jax.nn.log_sigmoid(gates[:, :, 1])
    h_f, st_f = mlstm_scan(q, k, v, logi[:, :, 0], logf[:, :, 0], *state_f)
    h_b, st_b = mlstm_scan(flip(q), flip(k), flip(v), flip(logi[:, :, 1]), flip(logf[:, :, 1]), *state_b)
    hs = h_f + flip(h_b)
    mu = hs.mean(-1, keepdims=True)
    var = jnp.square(hs - mu).mean(-1, keepdims=True)
    hn = ((hs - mu) * lax.rsqrt(var + LN_EPS)).reshape(B, T, vd) * norm_g
    y = (jax.nn.sigmoid(o.astype(jnp.float32)) * hn).astype(h.dtype) @ w_out
    return y, st_f, st_b


def axial_rope_tables(T):
    rows = T // GRID_W
    row = jnp.repeat(jnp.arange(rows, dtype=jnp.float32), GRID_W)
    col = jnp.tile(jnp.arange(GRID_W, dtype=jnp.float32), rows)
    freqs = ROPE_THETA ** (-jnp.arange(ROPE_PAIRS, dtype=jnp.float32) / ROPE_PAIRS)
    ang = jnp.stack([row[:, None] * freqs, col[:, None] * freqs], axis=1)
    return jnp.cos(ang), jnp.sin(ang)


def apply_rope(x, cos, sin):
    B, T, H, Dh = x.shape
    xr = x.astype(jnp.float32).reshape(B, T, H, 2, 2, ROPE_PAIRS)
    a, b = xr[..., 0, :], xr[..., 1, :]
    c, s = cos[None, :, None], sin[None, :, None]
    return jnp.stack([a * c - b * s, b * c + a * s], axis=-2).reshape(B, T, H, Dh).astype(x.dtype)


def blocked_attention(q, k, v):
    B, T, H, Dh = q.shape
    nb = T // Q_BLOCK
    qb = jnp.moveaxis(q.reshape(B, nb, Q_BLOCK, ATTN_KV_HEADS, ATTN_GROUP, Dh), 1, 0)

    def one_block(qblk):
        s = jnp.einsum('bqkgd,bskd->bkgqs', qblk, k).astype(jnp.float32) * (Dh ** -0.5)
        p = jax.nn.softmax(s, axis=-1).astype(v.dtype)
        return jnp.einsum('bkgqs,bskd->bqkgd', p, v)

    o = lax.map(one_block, qb)
    return jnp.moveaxis(o, 0, 1).reshape(B, T, H, Dh)


def attn_mixer(h, w_qkv, q_g, k_g, w_out, rope, ctx):
    B, T, _ = h.shape
    q, k, v = jnp.split(h @ w_qkv, [ATTN_HEADS * ATTN_HD, (ATTN_HEADS + ATTN_KV_HEADS) * ATTN_HD], axis=-1)
    q = rms_norm(q.reshape(B, T, ATTN_HEADS, ATTN_HD), q_g)
    k = rms_norm(k.reshape(B, T, ATTN_KV_HEADS, ATTN_HD), k_g)
    v = v.reshape(B, T, ATTN_KV_HEADS, ATTN_HD)
    if ctx is None:
        keys, vals = k, v
    else:
        cos, sin = rope
        q = apply_rope(q, cos, sin)
        keys = jnp.concatenate([apply_rope(k, cos, sin), ctx[0]], axis=1)
        vals = jnp.concatenate([v, ctx[1]], axis=1)
    o = blocked_attention(q, keys, vals)
    return o.reshape(B, T, ATTN_HEADS * ATTN_HD) @ w_out, k, v


def centred_dwconv(x, w, b):
    pad = SSD_CONV_W // 2
    y = lax.conv_general_dilated(x, w[:, None, :], window_strides=(1,), padding=[(pad, pad)],
                                 dimension_numbers=('NWC', 'WIO', 'NWC'), feature_group_count=x.shape[-1])
    return y + b


def ssd_scan(x, dt, A, Bm, Cm, h0):
    L = SSD_CHUNK
    Bsz, T = x.shape[:2]
    tril = jnp.tril(jnp.ones((L, L), bool))
    xdt = (x * dt[..., None]).reshape(Bsz, T, SSD_GROUPS, SSD_HPG, SSD_HD)
    dA = (dt * A).reshape(Bsz, T, SSD_GROUPS, SSD_HPG)

    def step(hst, inp):
        xc, ac, bc, cc = inp
        cs = jnp.cumsum(ac, axis=1)
        seg = cs[:, :, None] - cs[:, None, :]
        decay = jnp.exp(jnp.where(tril[None, :, :, None, None], seg, -jnp.inf))
        cb = jnp.einsum('btgn,bsgn->btsg', cc, bc)
        y_diag = jnp.einsum('btsge,bsgep->btgep', cb[..., None] * decay, xc)
        y_off = jnp.einsum('btgn,bgepn->btgep', cc, hst) * jnp.exp(cs)[..., None]
        tail = jnp.exp(cs[:, -1:] - cs)
        h_new = jnp.exp(cs[:, -1])[..., None, None] * hst + jnp.einsum('bsgep,bsgn->bgepn', tail[..., None] * xc, bc)
        return h_new, y_diag + y_off

    h0r = h0.astype(jnp.float32).reshape(Bsz, SSD_GROUPS, SSD_HPG, SSD_HD, SSD_N)
    hT, y = lax.scan(step, h0r, tuple(to_chunks(a, L) for a in (xdt, dA, Bm, Cm)))
    return from_chunks(y).reshape(Bsz, T, SSD_HEADS, SSD_HD), hT.reshape(Bsz, SSD_HEADS, SSD_HD, SSD_N)


def ssd_mixer(h, w_in, conv_w, conv_b, dt_bias, A_log, D_skip, norm_g, w_out, state_f, state_b):
    B, T, _ = h.shape
    gn = SSD_GROUPS * SSD_N
    z, xbc, dt = jnp.split(h @ w_in, [SSD_DI, SSD_DI + SSD_CONV_CH], axis=-1)
    xbc = jax.nn.silu(centred_dwconv(xbc, conv_w, conv_b))
    xs, Bm, Cm = jnp.split(xbc, [SSD_DI, SSD_DI + gn], axis=-1)
    xs = xs.reshape(B, T, SSD_HEADS, SSD_HD)
    Bm = Bm.reshape(B, T, SSD_GROUPS, SSD_N)
    Cm = Cm.reshape(B, T, SSD_GROUPS, SSD_N)
    dt = jax.nn.softplus(dt.astype(jnp.float32).reshape(B, T, 2, SSD_HEADS) + dt_bias)
    A = -jnp.exp(A_log.astype(jnp.float32))
    y_f, st_f = ssd_scan(xs, dt[:, :, 0], A[0], Bm, Cm, state_f)
    y_b, st_b = ssd_scan(flip(xs), flip(dt[:, :, 1]), A[1], flip(Bm), flip(Cm), state_b)
    y = y_f + flip(y_b) + D_skip[:, None] * xs
    y = y.reshape(B, T, SSD_DI) * jax.nn.silu(z.astype(jnp.float32))
    y = rms_norm(y, norm_g).astype(h.dtype)
    return y @ w_out, st_f, st_b


def setup_inputs(seed: int = 0) -> dict:
    key = jax.random.key(seed)
    ks = iter(jax.random.split(key, 48))

    def nrm(shape, scale=1.0):
        return jax.random.normal(next(ks), shape, jnp.float32) * scale

    def unif(shape, lo, hi):
        return jax.random.uniform(next(ks), shape, jnp.float32, lo, hi)

    dt0 = jnp.exp(unif((N_SSD_LAYERS, 2, SSD_HEADS), np.log(1e-3), np.log(1e-1)))
    return {
        'x_prompt': nrm((BATCH, SEQ, D_MODEL)),
        'x_sample': nrm((DEC_BATCH, DEC_SEQ, D_MODEL)),
        'cache_attn_k': nrm((DEC_BATCH, N_ATTN_LAYERS, PAST_LEN, ATTN_KV_HEADS, ATTN_HD)),
        'cache_attn_v': nrm((DEC_BATCH, N_ATTN_LAYERS, PAST_LEN, ATTN_KV_HEADS, ATTN_HD)),
        'state_mlstm_C': nrm((DEC_BATCH, N_MLSTM_LAYERS, 2, ML_HEADS, ML_DK, ML_DV), 0.1),
        'state_mlstm_n': nrm((DEC_BATCH, N_MLSTM_LAYERS, 2, ML_HEADS, ML_DK), 0.1),
        'state_mlstm_m': nrm((DEC_BATCH, N_MLSTM_LAYERS, 2, ML_HEADS)),
        'state_ssd_h': nrm((DEC_BATCH, N_SSD_LAYERS, 2, SSD_HEADS, SSD_HD, SSD_N), 0.1),
        'c': nrm((DEC_BATCH, D_MODEL)),
        'c_ctx': nrm((D_MODEL,)),
        'mod_w': nrm((DEPTH, D_MODEL, N_MOD * D_MODEL), D_MODEL ** -0.5),
        'mod_b': nrm((DEPTH, N_MOD * D_MODEL), 0.02),
        'ln_g': 1.0 + nrm((DEPTH, 3, D_MODEL), 0.02),
        'ln_b': nrm((DEPTH, 3, D_MODEL), 0.02),
        'ffn_w_in': nrm((DEPTH, 2, D_MODEL, 2 * D_FF), D_MODEL ** -0.5),
        'ffn_w_out': nrm((DEPTH, 2, D_FF, D_MODEL), BETA * D_FF ** -0.5),
        'mlstm_w_in': nrm((N_MLSTM_LAYERS, D_MODEL, ML_IN), D_MODEL ** -0.5),
        'mlstm_b_gate': jnp.concatenate([nrm((N_MLSTM_LAYERS, 2 * ML_HEADS), 0.1),
                                         unif((N_MLSTM_LAYERS, 2 * ML_HEADS), 3.0, 6.0)], axis=-1),
        'mlstm_norm_g': 1.0 + nrm((N_MLSTM_LAYERS, ML_HEADS * ML_DV), 0.02),
        'mlstm_w_out': nrm((N_MLSTM_LAYERS, ML_HEADS * ML_DV, D_MODEL), BETA * (ML_HEADS * ML_DV) ** -0.5),
        'attn_w_qkv': nrm((N_ATTN_LAYERS, D_MODEL, ATTN_IN), D_MODEL ** -0.5),
        'attn_q_norm': 1.0 + nrm((N_ATTN_LAYERS, ATTN_HD), 0.02),
        'attn_k_norm': 1.0 + nrm((N_ATTN_LAYERS, ATTN_HD), 0.02),
        'attn_w_out': nrm((N_ATTN_LAYERS, ATTN_HEADS * ATTN_HD, D_MODEL), BETA * (ATTN_HEADS * ATTN_HD) ** -0.5),
        'ssd_w_in': nrm((N_SSD_LAYERS, D_MODEL, SSD_IN), D_MODEL ** -0.5),
        'ssd_conv_w': nrm((N_SSD_LAYERS, SSD_CONV_W, SSD_CONV_CH), SSD_CONV_W ** -0.5),
        'ssd_conv_b': nrm((N_SSD_LAYERS, SSD_CONV_CH), 0.02),
        'ssd_dt_bias': dt0 + jnp.log(-jnp.expm1(-dt0)),
        'ssd_A_log': jnp.log(unif((N_SSD_LAYERS, 2, SSD_HEADS), 1.0, 16.0)),
        'ssd_D': 1.0 + nrm((N_SSD_LAYERS, SSD_HEADS), 0.1),
        'ssd_norm_g': 1.0 + nrm((N_SSD_LAYERS, SSD_DI), 0.02),
        'ssd_w_out': nrm((N_SSD_LAYERS, SSD_DI, D_MODEL), BETA * SSD_DI ** -0.5),
    }


def reference(x_prompt, x_sample, cache_attn_k, cache_attn_v, state_mlstm_C, state_mlstm_n,
              state_mlstm_m, state_ssd_h, c, c_ctx, mod_w, mod_b, ln_g, ln_b, ffn_w_in, ffn_w_out,
              mlstm_w_in, mlstm_b_gate, mlstm_norm_g, mlstm_w_out, attn_w_qkv, attn_q_norm,
              attn_k_norm, attn_w_out, ssd_w_in, ssd_conv_w, ssd_conv_b, ssd_dt_bias, ssd_A_log,
              ssd_D, ssd_norm_g, ssd_w_out):

    def run_mixer(i, h, ctx, rope):
        kind, j = i % N_MIXERS, i // N_MIXERS
        if kind == MIX_MLSTM:
            y, sf, sb = mlstm_mixer(h, mlstm_w_in[j], mlstm_b_gate[j], mlstm_norm_g[j], mlstm_w_out[j],
                                    ctx[0], ctx[1])
            return y, (sf, sb)
        if kind == MIX_ATTN:
            y, k, v = attn_mixer(h, attn_w_qkv[j], attn_q_norm[j], attn_k_norm[j], attn_w_out[j], rope, ctx)
            return y, (k, v)
        y, sf, sb = ssd_mixer(h, ssd_w_in[j], ssd_conv_w[j], ssd_conv_b[j], ssd_dt_bias[j], ssd_A_log[j],
                              ssd_D[j], ssd_norm_g[j], ssd_w_out[j], ctx[0], ctx[1])
        return y, (sf, sb)

    def run_layer(i, x, cvec, ctx, rope):
        m = adaln(cvec, mod_w[i], mod_b[i])
        x = ffn_step(x, m[0], m[1], m[2], ffn_w_in[i, 0], ffn_w_out[i, 0], ln_g[i, 0], ln_b[i, 0])
        y, st = run_mixer(i, modulate(x, m[3], m[4]), ctx, rope)
        x = post_norm(x, m[5] * y, ln_g[i, 1], ln_b[i, 1])
        x = ffn_step(x, m[6], m[7], m[8], ffn_w_in[i, 1], ffn_w_out[i, 1], ln_g[i, 2], ln_b[i, 2])
        return x, st

    Bp = x_prompt.shape[0]
    zero_ml = (jnp.zeros((Bp, ML_HEADS, ML_DK, ML_DV), jnp.float32),
               jnp.zeros((Bp, ML_HEADS, ML_DK), jnp.float32),
               jnp.zeros((Bp, ML_HEADS), jnp.float32))
    zero_ssd = jnp.zeros((Bp, SSD_HEADS, SSD_HD, SSD_N), jnp.float32)
    ctx_init = {MIX_MLSTM: (zero_ml, zero_ml), MIX_ATTN: None, MIX_SSD: (zero_ssd, zero_ssd)}
    cvec_ctx = c_ctx[None, None, :]
    x = x_prompt
    ml_states, attn_kv, ssd_states = [], [], []
    for i in range(DEPTH):
        kind = i % N_MIXERS
        x, st = run_layer(i, x, cvec_ctx, ctx_init[kind], None)
        if kind == MIX_MLSTM:
            ml_states.append(st)
        elif kind == MIX_ATTN:
            attn_kv.append(st)
        else:
            ssd_states.append(st)
    y_prompt = x
    dt_out = x_prompt.dtype
    new_cache_attn_k = jnp.stack([kv[0] for kv in attn_kv], axis=1)
    new_cache_attn_v = jnp.stack([kv[1] for kv in attn_kv], axis=1)
    new_state_mlstm_C = jnp.stack([jnp.stack([sf[0], sb[0]], axis=1) for sf, sb in ml_states], axis=1).astype(dt_out)
    new_state_mlstm_n = jnp.stack([jnp.stack([sf[1], sb[1]], axis=1) for sf, sb in ml_states], axis=1).astype(dt_out)
    new_state_mlstm_m = jnp.stack([jnp.stack([sf[2], sb[2]], axis=1) for sf, sb in ml_states], axis=1).astype(dt_out)
    new_state_ssd_h = jnp.stack([jnp.stack([sf, sb], axis=1) for sf, sb in ssd_states], axis=1).astype(dt_out)

    rope = axial_rope_tables(x_sample.shape[1])
    cvec = c[:, None, :]
    x = x_sample
    for i in range(DEPTH):
        kind, j = i % N_MIXERS, i // N_MIXERS
        if kind == MIX_MLSTM:
            ctx = ((state_mlstm_C[:, j, 0], state_mlstm_n[:, j, 0], state_mlstm_m[:, j, 0]),
                   (state_mlstm_C[:, j, 1], state_mlstm_n[:, j, 1], state_mlstm_m[:, j, 1]))
        elif kind == MIX_ATTN:
            ctx = (cache_attn_k[:, j], cache_attn_v[:, j])
        else:
            ctx = (state_ssd_h[:, j, 0], state_ssd_h[:, j, 1])
        x, _ = run_layer(i, x, cvec, ctx, rope)
    y_sample = x

    return (y_prompt, y_sample, new_cache_attn_k, new_cache_attn_v, new_state_mlstm_C,
            new_state_mlstm_n, new_state_mlstm_m, new_state_ssd_h)
```

```python
import functools
import math

import jax
import jax.numpy as jnp
from jax import lax
from jax.experimental import pallas as pl
from jax.experimental.pallas import tpu as pltpu

F32 = jnp.float32
BF16 = jnp.bfloat16

LN_EPS = 1e-6
N_MOD = 9
N_SEG_PAD = 8
LANES = 128
VMEM_LIMIT = 56 * 1024 * 1024

ML_HEADS, ML_DK, ML_DV, ML_CHUNK = 8, 128, 256, 128
ATTN_HEADS, ATTN_KV_HEADS, ATTN_HD = 16, 4, 128
ATTN_GROUP = ATTN_HEADS // ATTN_KV_HEADS
ATTN_QBLOCK = 256
GRID_W = 64
ROPE_THETA = 10000.0
ROPE_PAIRS = ATTN_HD // 4
SSD_HD, SSD_GROUPS, SSD_N, SSD_CONV_W, SSD_CHUNK = 64, 8, 128, 5, 128


def _params(*sem):
    return pltpu.CompilerParams(dimension_semantics=sem, vmem_limit_bytes=VMEM_LIMIT)


def _silu(x):
    return x * jax.nn.sigmoid(x)


def _log_sigmoid(x):
    return jnp.minimum(x, 0.0) - jnp.log1p(jnp.exp(-jnp.abs(x)))


def _softplus(x):
    return jnp.maximum(x, 0.0) + jnp.log1p(jnp.exp(-jnp.abs(x)))


def _adaln_kernel(cv_ref, w_ref, b_ref, o_ref):
    s = _silu(cv_ref[...]).astype(BF16)
    w = w_ref[...].astype(BF16)
    o_ref[...] = jnp.dot(s, w, preferred_element_type=F32) + b_ref[...]


def _adaln(cvec, mod_w, mod_b, bn=1024):
    depth, d, n = mod_w.shape
    return pl.pallas_call(
        _adaln_kernel,
        grid=(depth, n // bn),
        in_specs=[
            pl.BlockSpec((N_SEG_PAD, d), lambda l, j: (0, 0)),
            pl.BlockSpec((None, d, bn), lambda l, j: (l, 0, j)),
            pl.BlockSpec((None, 1, bn), lambda l, j: (l, 0, j)),
        ],
        out_specs=pl.BlockSpec((None, N_SEG_PAD, bn), lambda l, j: (l, 0, j)),
        out_shape=jax.ShapeDtypeStruct((depth, N_SEG_PAD, n), F32),
        compiler_params=_params("arbitrary", "arbitrary"),
        name="adaln",
    )(cvec, mod_w, mod_b.reshape(depth, 1, n))


class _Rows:
    def __init__(self, n_ctx_rows, dec_seq, d_model):
        self.n_ctx = n_ctx_rows
        self.dec_seq = dec_seq
        self.d = d_model

    def seg(self, i, bm):
        r = i * bm
        return jnp.where(r < self.n_ctx, 0, 1 + (r - self.n_ctx) // self.dec_seq)


def _mod_spec(rows, layer, k, bm):
    def idx(i, *_):
        return ((layer * N_SEG_PAD + rows.seg(i, bm)) * N_MOD + k, 0, 0)

    return pl.BlockSpec((None, 1, rows.d), idx)


def _modulate_kernel(x_ref, sh_ref, sc_ref, h_ref):
    h_ref[...] = (x_ref[...] * (1.0 + sc_ref[...]) + sh_ref[...]).astype(BF16)


def _modulate(x, modv, rows, layer, k_shift, k_scale, bm=512):
    m, d = x.shape
    return pl.pallas_call(
        _modulate_kernel,
        grid=(m // bm,),
        in_specs=[
            pl.BlockSpec((bm, d), lambda i: (i, 0)),
            _mod_spec(rows, layer, k_shift, bm),
            _mod_spec(rows, layer, k_scale, bm),
        ],
        out_specs=pl.BlockSpec((bm, d), lambda i: (i, 0)),
        out_shape=jax.ShapeDtypeStruct((m, d), BF16),
        compiler_params=_params("parallel"),
        name="modulate",
    )(x, modv, modv)


def _ffn_in_kernel(a_ref, wg_ref, wu_ref, o_ref, w_scr):
    bn = wg_ref.shape[1]

    @pl.when(pl.program_id(1) == 0)
    def _():
        w_scr[:, :bn] = wg_ref[...].astype(BF16)
        w_scr[:, bn:] = wu_ref[...].astype(BF16)

    u = jnp.dot(a_ref[...], w_scr[...], preferred_element_type=F32)
    o_ref[...] = (_silu(u[:, :bn]) * u[:, bn:]).astype(BF16)


def _ffn_in(a, w_in, bm=1024, bn=512):
    m, k = a.shape
    f = w_in.shape[1] // 2
    nt = f // bn
    return pl.pallas_call(
        _ffn_in_kernel,
        grid=(nt, m // bm),
        in_specs=[
            pl.BlockSpec((bm, k), lambda j, i: (i, 0)),
            pl.BlockSpec((k, bn), lambda j, i: (0, j)),
            pl.BlockSpec((k, bn), lambda j, i: (0, j + nt)),
        ],
        out_specs=pl.BlockSpec((bm, bn), lambda j, i: (i, j)),
        out_shape=jax.ShapeDtypeStruct((m, f), BF16),
        scratch_shapes=[pltpu.VMEM((k, 2 * bn), BF16)],
        compiler_params=_params("arbitrary", "arbitrary"),
        name="ffn_in",
    )(a, w_in, w_in)


def _proj_kernel(a_ref, w_ref, o_ref, w_scr, *, scale_first):
    @pl.when(pl.program_id(1) == 0)
    def _():
        w_scr[...] = w_ref[...].astype(BF16)

    u = jnp.dot(a_ref[...], w_scr[...], preferred_element_type=F32)
    if scale_first is not None:
        u = u * jnp.where(pl.program_id(0) == 0, scale_first, 1.0)
    o_ref[...] = u.astype(o_ref.dtype)


def _proj(a, w, col0, ncols, out_dtype, bm=1024, bn=1024, scale_first=None, name="proj"):
    m, k = a.shape
    assert col0 % bn == 0 and ncols % bn == 0
    j0 = col0 // bn
    return pl.pallas_call(
        functools.partial(_proj_kernel, scale_first=scale_first),
        grid=(ncols // bn, m // bm),
        in_specs=[
            pl.BlockSpec((bm, k), lambda j, i: (i, 0)),
            pl.BlockSpec((k, bn), lambda j, i: (0, j + j0)),
        ],
        out_specs=pl.BlockSpec((bm, bn), lambda j, i: (i, j)),
        out_shape=jax.ShapeDtypeStruct((m, ncols), out_dtype),
        scratch_shapes=[pltpu.VMEM((k, bn), BF16)],
        compiler_params=_params("arbitrary", "arbitrary"),
        name=name,
    )(a, w)


def _small_proj_kernel(a_ref, w_ref, o_ref):
    o_ref[...] = jnp.dot(a_ref[...], w_ref[...].astype(BF16), preferred_element_type=F32)


def _small_proj(a, w, col_block, ncols, bm=1024, name="small_proj"):
    m, k = a.shape
    return pl.pallas_call(
        _small_proj_kernel,
        grid=(m // bm,),
        in_specs=[
            pl.BlockSpec((bm, k), lambda i: (i, 0)),
            pl.BlockSpec((k, ncols), lambda i: (0, col_block)),
        ],
        out_specs=pl.BlockSpec((bm, ncols), lambda i: (i, 0)),
        out_shape=jax.ShapeDtypeStruct((m, ncols), F32),
        compiler_params=_params("parallel"),
        name=name,
    )(a, w)


def _out_ln_kernel(*refs, alpha, coef, has_next, has_ssq, ssq_dim):
    it = iter(refs)
    a_ref, w_ref, x_ref, gate_ref, g_ref, b_ref = (next(it) for _ in range(6))
    sh_ref = sc_ref = ssq_ref = None
    if has_next:
        sh_ref, sc_ref = next(it), next(it)
    if has_ssq:
        ssq_ref = next(it)
    xo_ref = next(it)
    ho_ref = next(it) if has_next else None
    acc = next(it)

    k = pl.program_id(1)

    @pl.when(k == 0)
    def _():
        acc[...] = jnp.zeros_like(acc)

    acc[...] += jnp.dot(a_ref[...], w_ref[...].astype(BF16), preferred_element_type=F32)

    @pl.when(k == pl.num_programs(1) - 1)
    def _():
        y = acc[...]
        if has_ssq:
            ms = jnp.sum(ssq_ref[...], axis=1, keepdims=True) * (1.0 / ssq_dim)
            y = y * lax.rsqrt(ms + LN_EPS)
        z = alpha * x_ref[...] + (coef * gate_ref[...]) * y
        mu = jnp.mean(z, axis=1, keepdims=True)
        zc = z - mu
        var = jnp.mean(zc * zc, axis=1, keepdims=True)
        xn = zc * lax.rsqrt(var + LN_EPS) * g_ref[...] + b_ref[...]
        xo_ref[...] = xn
        if has_next:
            ho_ref[...] = (xn * (1.0 + sc_ref[...]) + sh_ref[...]).astype(BF16)


def _out_ln(a, w, x, modv, lnv, rows, *, layer, k_gate, ln_idx, alpha, coef,
            nxt=None, ssq=None, ssq_dim=None, bm=512, bk=512, name="out_ln"):
    m, kdim = a.shape
    d = x.shape[1]
    assert kdim % bk == 0
    in_specs = [
        pl.BlockSpec((bm, bk), lambda i, k: (i, k)),
        pl.BlockSpec((bk, d), lambda i, k: (k, 0)),
        pl.BlockSpec((bm, d), lambda i, k: (i, 0)),
        _mod_spec(rows, layer, k_gate, bm),
        pl.BlockSpec((None, 1, d), lambda i, k: (2 * ln_idx, 0, 0)),
        pl.BlockSpec((None, 1, d), lambda i, k: (2 * ln_idx + 1, 0, 0)),
    ]
    args = [a, w, x, modv, lnv, lnv]
    if nxt is not None:
        in_specs += [_mod_spec(rows, nxt[0], nxt[1], bm), _mod_spec(rows, nxt[0], nxt[2], bm)]
        args += [modv, modv]
    if ssq is not None:
        in_specs.append(pl.BlockSpec((bm, ssq.shape[1]), lambda i, k: (i, 0)))
        args.append(ssq)
    out_specs = [pl.BlockSpec((bm, d), lambda i, k: (i, 0))]
    out_shape = [jax.ShapeDtypeStruct((m, d), F32)]
    if nxt is not None:
        out_specs.append(pl.BlockSpec((bm, d), lambda i, k: (i, 0)))
        out_shape.append(jax.ShapeDtypeStruct((m, d), BF16))
    res = pl.pallas_call(
        functools.partial(_out_ln_kernel, alpha=alpha, coef=coef, has_next=nxt is not None,
                          has_ssq=ssq is not None, ssq_dim=ssq_dim),
        grid=(m // bm, kdim // bk),
        in_specs=in_specs,
        out_specs=out_specs,
        out_shape=out_shape,
        scratch_shapes=[pltpu.VMEM((bm, d), F32)],
        compiler_params=_params("parallel", "arbitrary"),
        name=name,
    )(*args)
    return (res[0], res[1]) if nxt is not None else (res[0], None)


def _mlstm_kernel(*refs, n_chunks, has_init, has_state_out):
    it = iter(refs)
    q_ref, k_ref, v_ref, o_ref, gc_ref, gr_ref, bc_ref, br_ref, ng_ref = (next(it) for _ in range(9))
    c0_ref = n0_ref = m0_ref = None
    if has_init:
        c0_ref, n0_ref, m0_ref = next(it), next(it), next(it)
    y_ref = next(it)
    co_ref = no_ref = mo_ref = None
    if has_state_out:
        co_ref, no_ref, mo_ref = next(it), next(it), next(it)
    hs_scr, c_scr, n_scr, m_scr = next(it), next(it), next(it), next(it)

    L = ML_CHUNK
    t_idx = lax.broadcasted_iota(jnp.int32, (L, L), 0)
    s_idx = lax.broadcasted_iota(jnp.int32, (L, L), 1)

    for d in range(2):
        incl = (s_idx <= t_idx) if d == 0 else (s_idx >= t_idx)
        incl_t = (t_idx <= s_idx) if d == 0 else (t_idx >= s_idx)
        if has_init:
            c_scr[...] = c0_ref[d]
            n_scr[...] = n0_ref[d]
            m_scr[...] = m0_ref[d]
        else:
            c_scr[...] = jnp.zeros_like(c_scr)
            n_scr[...] = jnp.zeros_like(n_scr)
            m_scr[...] = jnp.zeros_like(m_scr)

        def chunk(ci, carry, d=d, incl=incl, incl_t=incl_t):
            c = ci if d == 0 else n_chunks - 1 - ci
            r0 = pl.multiple_of(c * L, L)
            qc = q_ref[pl.ds(r0, L), :]
            kc = k_ref[pl.ds(r0, L), :]
            vc = v_ref[pl.ds(r0, L), :]
            gcol = gc_ref[pl.ds(r0, L), :] + bc_ref[...]
            grow = gr_ref[c] + br_ref[...]
            li_c = gcol[:, d:d + 1]
            lf_c = _log_sigmoid(gcol[:, 2 + d:3 + d])
            li_r = grow[d:d + 1, :]
            lf_r = _log_sigmoid(grow[2 + d:3 + d, :])
            b_c = jnp.sum(jnp.where(incl, lf_r, 0.0), axis=1, keepdims=True)
            b_r = jnp.sum(jnp.where(incl_t, lf_c, 0.0), axis=0, keepdims=True)
            g = jnp.sum(lf_r, axis=1, keepdims=True)
            m_prev = m_scr[...]
            dmat = jnp.where(incl, b_c - b_r + li_r, -jnp.inf)
            inter = b_c + m_prev
            m_t = jnp.maximum(inter, jnp.max(dmat, axis=1, keepdims=True))
            w = jnp.exp(dmat - m_t)
            qk = lax.dot_general(qc, kc, (((1,), (1,)), ((), ())), preferred_element_type=F32)
            s = qk * w
            sc = jnp.exp(inter - m_t)
            q_c = jnp.dot(qc, c_scr[...].astype(BF16), preferred_element_type=F32)
            num = sc * q_c + jnp.dot(s.astype(BF16), vc, preferred_element_type=F32)
            qn = jnp.sum(qc.astype(F32) * n_scr[...], axis=1, keepdims=True)
            den = sc * qn + jnp.sum(s, axis=1, keepdims=True)
            h = num / jnp.maximum(jnp.abs(den), jnp.exp(-m_t))
            if d == 0:
                hs_scr[pl.ds(r0, L), :] = h
            else:
                hs_scr[pl.ds(r0, L), :] += h
            ds_c = g - b_c + li_c
            ds_r = g - b_r + li_r
            m_new = jnp.maximum(g + m_prev, jnp.max(ds_r, axis=1, keepdims=True))
            wk = jnp.exp(ds_c - m_new) * kc.astype(F32)
            decay = jnp.exp(g + m_prev - m_new)
            c_scr[...] = decay * c_scr[...] + lax.dot_general(
                wk.astype(BF16), vc, (((0,), (0,)), ((), ())), preferred_element_type=F32)
            n_scr[...] = decay * n_scr[...] + jnp.sum(wk, axis=0, keepdims=True)
            m_scr[...] = m_new
            return carry

        lax.fori_loop(0, n_chunks, chunk, 0)
        if has_state_out:
            co_ref[d] = c_scr[...]
            no_ref[d] = n_scr[...]
            mo_ref[d] = m_scr[...]

    hs = hs_scr[...]
    mu = jnp.mean(hs, axis=1, keepdims=True)
    hc = hs - mu
    var = jnp.mean(hc * hc, axis=1, keepdims=True)
    hn = hc * lax.rsqrt(var + LN_EPS) * ng_ref[...]
    y_ref[...] = (jax.nn.sigmoid(o_ref[...].astype(F32)) * hn).astype(BF16)


def _mlstm_core(qkv, o, gates, b_gate, norm_g, *, batch, seq, row_blk0, init=None, state_out=False):
    m = qkv.shape[0]
    h, dk, dv, L = ML_HEADS, ML_DK, ML_DV, ML_CHUNK
    nc = seq // L
    r0 = row_blk0 * seq
    g4 = gates[r0:r0 + batch * seq].reshape(batch, seq, 4, h)
    gcol = jnp.transpose(g4, (0, 3, 1, 2))
    grow = jnp.transpose(g4.reshape(batch, nc, L, 4, h), (0, 4, 1, 3, 2))
    b4 = b_gate.reshape(4, h)
    bcol = jnp.transpose(b4)[:, None, :]
    brow = jnp.transpose(b4)[:, :, None]
    kq = (h * dk) // dk
    kv = (2 * h * dk) // dv
    in_specs = [
        pl.BlockSpec((seq, dk), lambda b, hh: (row_blk0 + b, hh)),
        pl.BlockSpec((seq, dk), lambda b, hh: (row_blk0 + b, kq + hh)),
        pl.BlockSpec((seq, dv), lambda b, hh: (row_blk0 + b, kv + hh)),
        pl.BlockSpec((seq, dv), lambda b, hh: (row_blk0 + b, hh)),
        pl.BlockSpec((None, None, seq, 4), lambda b, hh: (b, hh, 0, 0)),
        pl.BlockSpec((None, None, nc, 4, L), lambda b, hh: (b, hh, 0, 0, 0)),
        pl.BlockSpec((None, 1, 4), lambda b, hh: (hh, 0, 0)),
        pl.BlockSpec((None, 4, 1), lambda b, hh: (hh, 0, 0)),
        pl.BlockSpec((1, dv), lambda b, hh: (0, hh)),
    ]
    args = [qkv, qkv, qkv, o, gcol, grow, bcol, brow, norm_g.reshape(1, h * dv)]
    if init is not None:
        c0, n0, m0 = init
        in_specs += [
            pl.BlockSpec((None, 2, None, dk, dv), lambda b, hh: (b, 0, hh, 0, 0)),
            pl.BlockSpec((None, 2, None, 1, dk), lambda b, hh: (b, 0, hh, 0, 0)),
            pl.BlockSpec((None, 2, None, 1, 1), lambda b, hh: (b, 0, hh, 0, 0)),
        ]
        args += [c0, n0.reshape(batch, 2, h, 1, dk), m0.reshape(batch, 2, h, 1, 1)]
    out_specs = [pl.BlockSpec((seq, dv), lambda b, hh: (row_blk0 + b, hh))]
    out_shape = [jax.ShapeDtypeStruct((m, h * dv), BF16)]
    if state_out:
        out_specs += [
            pl.BlockSpec((None, 2, None, dk, dv), lambda b, hh: (b, 0, hh, 0, 0)),
            pl.BlockSpec((None, 2, None, 1, dk), lambda b, hh: (b, 0, hh, 0, 0)),
            pl.BlockSpec((None, 2, None, 1, 1), lambda b, hh: (b, 0, hh, 0, 0)),
        ]
        out_shape += [
            jax.ShapeDtypeStruct((batch, 2, h, dk, dv), F32),
            jax.ShapeDtypeStruct((batch, 2, h, 1, dk), F32),
            jax.ShapeDtypeStruct((batch, 2, h, 1, 1), F32),
        ]
    return pl.pallas_call(
        functools.partial(_mlstm_kernel, n_chunks=nc, has_init=init is not None, has_state_out=state_out),
        grid=(batch, h),
        in_specs=in_specs,
        out_specs=out_specs,
        out_shape=out_shape,
        scratch_shapes=[
            pltpu.VMEM((seq, dv), F32),
            pltpu.VMEM((dk, dv), F32),
            pltpu.VMEM((1, dk), F32),
            pltpu.VMEM((1, 1), F32),
        ],
        compiler_params=_params("parallel", "parallel"),
        name="mlstm_core",
    )(*args)


def _rope(x, cos_t, sin_t):
    lane = lax.broadcasted_iota(jnp.int32, x.shape, 1)
    first = (lane % (2 * ROPE_PAIRS)) < ROPE_PAIRS
    rot = jnp.where(first, pltpu.roll(x, x.shape[1] - ROPE_PAIRS, 1), pltpu.roll(x, ROPE_PAIRS, 1))
    return x * cos_t + rot * sin_t


def _rms(x, g):
    return x * lax.rsqrt(jnp.mean(x * x, axis=1, keepdims=True) + LN_EPS) * g


def _attn_kernel(*refs, seq, has_ctx):
    it = iter(refs)
    q_ref, k_ref, v_ref, qg_ref, kg_ref = (next(it) for _ in range(5))
    cos_ref = sin_ref = ck_ref = cv_ref = None
    if has_ctx:
        cos_ref, sin_ref, ck_ref, cv_ref = (next(it) for _ in range(4))
    o_ref = next(it)
    ko_ref = vo_ref = None
    if not has_ctx:
        ko_ref, vo_ref = next(it), next(it)
    keys_scr, vals_scr = next(it), next(it)

    hd = ATTN_HD
    kn = _rms(k_ref[...].astype(F32), kg_ref[...])
    if has_ctx:
        keys_scr[:seq, :] = _rope(kn, cos_ref[...], sin_ref[...]).astype(BF16)
        keys_scr[seq:, :] = ck_ref[...].astype(BF16)
        vals_scr[:seq, :] = v_ref[...]
        vals_scr[seq:, :] = cv_ref[...].astype(BF16)
    else:
        ko_ref[...] = kn
        vo_ref[...] = v_ref[...].astype(F32)
        keys_scr[...] = kn.astype(BF16)
        vals_scr[...] = v_ref[...]

    qb = min(ATTN_QBLOCK, seq)
    scale = hd ** -0.5

    def qblock(bi, carry):
        r0 = pl.multiple_of(bi * qb, qb)
        for g in range(ATTN_GROUP):
            q = _rms(q_ref[pl.ds(r0, qb), g * hd:(g + 1) * hd].astype(F32), qg_ref[...])
            if has_ctx:
                q = _rope(q, cos_ref[pl.ds(r0, qb), :], sin_ref[pl.ds(r0, qb), :])
            q = (q * scale).astype(BF16)
            s = lax.dot_general(q, keys_scr[...], (((1,), (1,)), ((), ())), preferred_element_type=F32)
            mx = jnp.max(s, axis=1, keepdims=True)
            p = jnp.exp(s - mx)
            den = jnp.sum(p, axis=1, keepdims=True)
            o = jnp.dot(p.astype(BF16), vals_scr[...], preferred_element_type=F32) / den
            o_ref[pl.ds(r0, qb), g * hd:(g + 1) * hd] = o.astype(BF16)
        return carry

    lax.fori_loop(0, seq // qb, qblock, 0)


def _attn_core(qkv, q_g, k_g, *, batch, seq, row_blk0, rope=None, ctx=None):
    m = qkv.shape[0]
    hd, kvh, grp = ATTN_HD, ATTN_KV_HEADS, ATTN_GROUP
    has_ctx = ctx is not None
    k_off = ATTN_HEADS
    v_off = ATTN_HEADS + kvh
    in_specs = [
        pl.BlockSpec((seq, grp * hd), lambda b, kv: (row_blk0 + b, kv)),
        pl.BlockSpec((seq, hd), lambda b, kv: (row_blk0 + b, k_off + kv)),
        pl.BlockSpec((seq, hd), lambda b, kv: (row_blk0 + b, v_off + kv)),
        pl.BlockSpec((1, hd), lambda b, kv: (0, 0)),
        pl.BlockSpec((1, hd), lambda b, kv: (0, 0)),
    ]
    args = [qkv, qkv, qkv, q_g.reshape(1, hd), k_g.reshape(1, hd)]
    n_keys = seq
    if has_ctx:
        cos_t, sin_t = rope
        ck, cv = ctx
        past = ck.shape[1]
        n_keys = seq + past
        in_specs += [
            pl.BlockSpec((seq, hd), lambda b, kv: (0, 0)),
            pl.BlockSpec((seq, hd), lambda b, kv: (0, 0)),
            pl.BlockSpec((None, past, hd), lambda b, kv: (b, 0, kv)),
            pl.BlockSpec((None, past, hd), lambda b, kv: (b, 0, kv)),
        ]
        args += [cos_t, sin_t, ck, cv]
    out_specs = [pl.BlockSpec((seq, grp * hd), lambda b, kv: (row_blk0 + b, kv))]
    out_shape = [jax.ShapeDtypeStruct((m, ATTN_HEADS * hd), BF16)]
    if not has_ctx:
        out_specs += [pl.BlockSpec((None, seq, hd), lambda b, kv: (b, 0, kv))] * 2
        out_shape += [jax.ShapeDtypeStruct((batch, seq, kvh * hd), F32)] * 2
    return pl.pallas_call(
        functools.partial(_attn_kernel, seq=seq, has_ctx=has_ctx),
        grid=(batch, kvh),
        in_specs=in_specs,
        out_specs=out_specs,
        out_shape=out_shape,
        scratch_shapes=[pltpu.VMEM((n_keys, hd), BF16), pltpu.VMEM((n_keys, hd), BF16)],
        compiler_params=_params("parallel", "parallel"),
        name="attn_core",
    )(*args)


def _rope_tables(seq):
    rows = seq // GRID_W
    row = jnp.repeat(jnp.arange(rows, dtype=F32), GRID_W)
    col = jnp.tile(jnp.arange(GRID_W, dtype=F32), rows)
    freqs = ROPE_THETA ** (-jnp.arange(ROPE_PAIRS, dtype=F32) / ROPE_PAIRS)
    ar, ac = row[:, None] * freqs, col[:, None] * freqs
    cos_t = jnp.concatenate([jnp.cos(ar), jnp.cos(ar), jnp.cos(ac), jnp.cos(ac)], axis=1)
    sin_t = jnp.concatenate([-jnp.sin(ar), jnp.sin(ar), -jnp.sin(ac), jnp.sin(ac)], axis=1)
    return cos_t, sin_t


def _dwconv_silu(x, w, b):
    t = x.shape[0]
    row = lax.broadcasted_iota(jnp.int32, x.shape, 0)
    pad = SSD_CONV_W // 2
    acc = jnp.zeros_like(x) + b
    for j in range(SSD_CONV_W):
        off = j - pad
        if off == 0:
            xs = x
        else:
            xs = pltpu.roll(x, (-off) % t, 0)
            valid = (row + off >= 0) & (row + off < t)
            xs = jnp.where(valid, xs, 0.0)
        acc = acc + xs * w[j:j + 1, :]
    return _silu(acc)


def _ssd_kernel(*refs, n_chunks, has_init, has_state_out):
    it = iter(refs)
    (z_ref, x_ref, b_ref, c_ref, wx_ref, wb_ref, wc_ref, bx_ref, bb_ref, bc_ref,
     dtc_ref, dtr_ref, dbc_ref, dbr_ref, alc_ref, alr_ref, dsk_ref, ng_ref) = (next(it) for _ in range(18))
    h0_ref = next(it) if has_init else None
    y_ref, ssq_ref = next(it), next(it)
    ho_ref = next(it) if has_state_out else None
    xs_scr, bm_scr, cm_scr, y_scr, h_scr = (next(it) for _ in range(5))

    L, P, E = SSD_CHUNK, SSD_HD, x_ref.shape[1] // SSD_HD

    xs_scr[...] = _dwconv_silu(x_ref[...].astype(F32), wx_ref[...], bx_ref[...])
    bm_scr[...] = _dwconv_silu(b_ref[...].astype(F32), wb_ref[...], bb_ref[...]).astype(BF16)
    cm_scr[...] = _dwconv_silu(c_ref[...].astype(F32), wc_ref[...], bc_ref[...]).astype(BF16)

    t_idx = lax.broadcasted_iota(jnp.int32, (L, L), 0)
    s_idx = lax.broadcasted_iota(jnp.int32, (L, L), 1)

    for d in range(2):
        incl = (s_idx <= t_idx) if d == 0 else (s_idx >= t_idx)
        incl_t = (t_idx <= s_idx) if d == 0 else (t_idx >= s_idx)
        a_c = -jnp.exp(alc_ref[...][:, d * E:(d + 1) * E])
        a_r = -jnp.exp(alr_ref[...][d * E:(d + 1) * E, :])
        if has_init:
            h_scr[...] = h0_ref[d]
        else:
            h_scr[...] = jnp.zeros_like(h_scr)

        def chunk(ci, carry, d=d, incl=incl, incl_t=incl_t, a_c=a_c, a_r=a_r):
            c = ci if d == 0 else n_chunks - 1 - ci
            r0 = pl.multiple_of(c * L, L)
            bc = bm_scr[pl.ds(r0, L), :]
            cc = cm_scr[pl.ds(r0, L), :]
            dt_c = _softplus(dtc_ref[pl.ds(r0, L), :][:, d * E:(d + 1) * E]
                             + dbc_ref[...][:, d * E:(d + 1) * E])
            dt_r = _softplus(dtr_ref[c][d * E:(d + 1) * E, :]
                             + dbr_ref[...][d * E:(d + 1) * E, :])
            da_c = dt_c * a_c
            da_r = dt_r * a_r
            cb = lax.dot_general(cc, bc, (((1,), (1,)), ((), ())), preferred_element_type=F32)
            ys = []
            for e in range(E):
                cs_c = jnp.sum(jnp.where(incl, da_r[e:e + 1, :], 0.0), axis=1, keepdims=True)
                cs_r = jnp.sum(jnp.where(incl_t, da_c[:, e:e + 1], 0.0), axis=0, keepdims=True)
                tot = jnp.sum(da_r[e:e + 1, :], axis=1, keepdims=True)
                decay = jnp.exp(jnp.where(incl, cs_c - cs_r, -jnp.inf))
                xdt = xs_scr[pl.ds(r0, L), e * P:(e + 1) * P] * dt_c[:, e:e + 1]
                y_d = jnp.dot((cb * decay).astype(BF16), xdt.astype(BF16), preferred_element_type=F32)
                h_e = h_scr[e]
                y_o = lax.dot_general(cc, h_e.astype(BF16), (((1,), (1,)), ((), ())),
                                      preferred_element_type=F32) * jnp.exp(cs_c)
                ys.append(y_d + y_o)
                tail = jnp.exp(tot - cs_c)
                h_scr[e] = jnp.exp(tot) * h_e + lax.dot_general(
                    (tail * xdt).astype(BF16), bc, (((0,), (0,)), ((), ())), preferred_element_type=F32)
            y = jnp.concatenate(ys, axis=1)
            if d == 0:
                y_scr[pl.ds(r0, L), :] = y
            else:
                y_scr[pl.ds(r0, L), :] += y
            return carry

        lax.fori_loop(0, n_chunks, chunk, 0)
        if has_state_out:
            ho_ref[d] = h_scr[...]

    y = y_scr[...] + dsk_ref[...] * xs_scr[...]
    yz = y * _silu(z_ref[...].astype(F32))
    ssq_ref[...] = jnp.sum(yz * yz, axis=1, keepdims=True)
    y_ref[...] = (yz * ng_ref[...]).astype(BF16)


def _ssd_core(zxbc, dt_raw, conv_w, conv_b, dt_bias, a_log, d_skip, norm_g, *,
              batch, seq, row_blk0, init=None, state_out=False):
    m = zxbc.shape[0]
    G, P, N, L = SSD_GROUPS, SSD_HD, SSD_N, SSD_CHUNK
    di = d_skip.shape[0] * P
    heads = di // P
    E = heads // G
    gw = E * P
    nc = seq // L
    r0 = row_blk0 * seq
    dt4 = dt_raw[r0:r0 + batch * seq].reshape(batch, seq, 2, G, E)
    dtc = jnp.transpose(dt4, (0, 3, 1, 2, 4)).reshape(batch, G, seq, 2 * E)
    dtr = jnp.transpose(dt4.reshape(batch, nc, L, 2, G, E), (0, 4, 1, 3, 5, 2)).reshape(batch, G, nc, 2 * E, L)

    def per_group(v):
        v3 = jnp.transpose(v.reshape(2, G, E), (1, 0, 2)).reshape(G, 2 * E)
        return v3[:, None, :], v3[:, :, None]

    dbc, dbr = per_group(dt_bias)
    alc, alr = per_group(a_log)
    x_blk0 = di // gw
    b_blk0 = (2 * di) // N
    c_blk0 = (2 * di + G * N) // N
    in_specs = [
        pl.BlockSpec((seq, gw), lambda b, g: (row_blk0 + b, g)),
        pl.BlockSpec((seq, gw), lambda b, g: (row_blk0 + b, x_blk0 + g)),
        pl.BlockSpec((seq, N), lambda b, g: (row_blk0 + b, b_blk0 + g)),
        pl.BlockSpec((seq, N), lambda b, g: (row_blk0 + b, c_blk0 + g)),
        pl.BlockSpec((SSD_CONV_W, gw), lambda b, g: (0, g)),
        pl.BlockSpec((SSD_CONV_W, N), lambda b, g: (0, di // N + g)),
        pl.BlockSpec((SSD_CONV_W, N), lambda b, g: (0, di // N + G + g)),
        pl.BlockSpec((1, gw), lambda b, g: (0, g)),
        pl.BlockSpec((1, N), lambda b, g: (0, di // N + g)),
        pl.BlockSpec((1, N), lambda b, g: (0, di // N + G + g)),
        pl.BlockSpec((None, None, seq, 2 * E), lambda b, g: (b, g, 0, 0)),
        pl.BlockSpec((None, None, nc, 2 * E, L), lambda b, g: (b, g, 0, 0, 0)),
        pl.BlockSpec((None, 1, 2 * E), lambda b, g: (g, 0, 0)),
        pl.BlockSpec((None, 2 * E, 1), lambda b, g: (g, 0, 0)),
        pl.BlockSpec((None, 1, 2 * E), lambda b, g: (g, 0, 0)),
        pl.BlockSpec((None, 2 * E, 1), lambda b, g: (g, 0, 0)),
        pl.BlockSpec((1, gw), lambda b, g: (0, g)),
        pl.BlockSpec((1, gw), lambda b, g: (0, g)),
    ]
    cb2 = conv_b.reshape(1, -1)
    args = [zxbc, zxbc, zxbc, zxbc, conv_w, conv_w, conv_w, cb2, cb2, cb2,
            dtc, dtr, dbc, dbr, alc, alr,
            jnp.repeat(d_skip, P).reshape(1, di), norm_g.reshape(1, di)]
    if init is not None:
        in_specs.append(pl.BlockSpec((None, 2, E, P, N), lambda b, g: (b, 0, g, 0, 0)))
        args.append(init)
    out_specs = [
        pl.BlockSpec((seq, gw), lambda b, g: (row_blk0 + b, g)),
        pl.BlockSpec((None, seq, 1), lambda b, g: (g, row_blk0 + b, 0)),
    ]
    out_shape = [jax.ShapeDtypeStruct((m, di), BF16), jax.ShapeDtypeStruct((G, m, 1), F32)]
    if state_out:
        out_specs.append(pl.BlockSpec((None, 2, E, P, N), lambda b, g: (b, 0, g, 0, 0)))
        out_shape.append(jax.ShapeDtypeStruct((batch, 2, heads, P, N), F32))
    return pl.pallas_call(
        functools.partial(_ssd_kernel, n_chunks=nc, has_init=init is not None, has_state_out=state_out),
        grid=(batch, G),
        in_specs=in_specs,
        out_specs=out_specs,
        out_shape=out_shape,
        scratch_shapes=[
            pltpu.VMEM((seq, gw), F32),
            pltpu.VMEM((seq, N), BF16),
            pltpu.VMEM((seq, N), BF16),
            pltpu.VMEM((seq, gw), F32),
            pltpu.VMEM((E, P, N), F32),
        ],
        compiler_params=_params("parallel", "parallel"),
        name="ssd_core",
    )(*args)


def kernel(x_prompt, x_sample, cache_attn_k, cache_attn_v, state_mlstm_C, state_mlstm_n, state_mlstm_m, state_ssd_h, c, c_ctx, mod_w, mod_b, ln_g, ln_b, ffn_w_in, ffn_w_out, mlstm_w_in, mlstm_b_gate, mlstm_norm_g, mlstm_w_out, attn_w_qkv, attn_q_norm, attn_k_norm, attn_w_out, ssd_w_in, ssd_conv_w, ssd_conv_b, ssd_dt_bias, ssd_A_log, ssd_D, ssd_norm_g, ssd_w_out):
    bp, sp, d = x_prompt.shape
    bd, sd, _ = x_sample.shape
    depth = mod_w.shape[0]
    n_ctx = bp * sp
    m = n_ctx + bd * sd
    rows = _Rows(n_ctx, sd, d)
    alpha = (2 * depth) ** 0.25
    ctx_blk0, dec_blk0 = 0, n_ctx // sd

    cvec = jnp.zeros((N_SEG_PAD, d), F32).at[0].set(c_ctx).at[1:1 + bd].set(c)
    modv = _adaln(cvec, mod_w, mod_b).reshape(depth * N_SEG_PAD * N_MOD, 1, d)
    lnv = jnp.stack([ln_g, ln_b], axis=2).reshape(depth * 3 * 2, 1, d)

    x = jnp.concatenate([x_prompt.reshape(n_ctx, d), x_sample.reshape(bd * sd, d)], axis=0)
    h = _modulate(x, modv, rows, 0, 0, 1)

    ml_states, attn_kv, ssd_states = [], [], []
    for i in range(depth):
        kind, j = i % 3, i // 3
        act = _ffn_in(h, ffn_w_in[i, 0])
        x, h = _out_ln(act, ffn_w_out[i, 0], x, modv, lnv, rows, layer=i, k_gate=2, ln_idx=3 * i,
                       alpha=alpha, coef=0.5, nxt=(i, 3, 4), name="ffn_out")
        ssq = None
        if kind == 0:
            w_in = mlstm_w_in[j]
            qk_w, v_w = 2 * ML_HEADS * ML_DK, ML_HEADS * ML_DV
            qkv = _proj(h, w_in, 0, qk_w + v_w, BF16, scale_first=ML_DK ** -0.5, bn=ML_HEADS * ML_DK,
                        name="mlstm_qkv")
            og = _proj(h, w_in, qk_w + v_w, v_w, BF16, name="mlstm_o")
            gates = _small_proj(h, w_in[:, qk_w + 2 * v_w:], 0, 4 * ML_HEADS, name="mlstm_gates")
            y_c, cs, ns, ms = _mlstm_core(qkv, og, gates, mlstm_b_gate[j], mlstm_norm_g[j],
                                          batch=bp, seq=sp, row_blk0=ctx_blk0, state_out=True)
            (y_d,) = _mlstm_core(qkv, og, gates, mlstm_b_gate[j], mlstm_norm_g[j],
                                 batch=bd, seq=sd, row_blk0=dec_blk0,
                                 init=(state_mlstm_C[:, j], state_mlstm_n[:, j], state_mlstm_m[:, j]))
            ml_states.append((cs, ns.reshape(bp, 2, ML_HEADS, ML_DK), ms.reshape(bp, 2, ML_HEADS)))
            w_out = mlstm_w_out[j]
        elif kind == 1:
            qkv = _proj(h, attn_w_qkv[j], 0, attn_w_qkv.shape[2], BF16, name="attn_qkv")
            y_c, kc, vc = _attn_core(qkv, attn_q_norm[j], attn_k_norm[j], batch=bp, seq=sp, row_blk0=ctx_blk0)
            past = cache_attn_k.shape[2]
            (y_d,) = _attn_core(qkv, attn_q_norm[j], attn_k_norm[j], batch=bd, seq=sd, row_blk0=dec_blk0,
                                rope=_rope_tables(sd),
                                ctx=(cache_attn_k[:, j].reshape(bd, past, -1), cache_attn_v[:, j].reshape(bd, past, -1)))
            attn_kv.append((kc.reshape(bp, sp, ATTN_KV_HEADS, ATTN_HD), vc.reshape(bp, sp, ATTN_KV_HEADS, ATTN_HD)))
            w_out = attn_w_out[j]
        else:
            w_in = ssd_w_in[j]
            di = ssd_D.shape[1] * SSD_HD
            zxbc_w = 2 * di + 2 * SSD_GROUPS * SSD_N
            zxbc = _proj(h, w_in, 0, zxbc_w, BF16, name="ssd_zxbc")
            dt_raw = _small_proj(h, w_in, zxbc_w // LANES, LANES, name="ssd_dt")
            y_c, ssq_c, hst = _ssd_core(zxbc, dt_raw, ssd_conv_w[j], ssd_conv_b[j], ssd_dt_bias[j], ssd_A_log[j],
                                        ssd_D[j], ssd_norm_g[j], batch=bp, seq=sp, row_blk0=ctx_blk0, state_out=True)
            y_d, ssq_d = _ssd_core(zxbc, dt_raw, ssd_conv_w[j], ssd_conv_b[j], ssd_dt_bias[j], ssd_A_log[j],
                                   ssd_D[j], ssd_norm_g[j], batch=bd, seq=sd, row_blk0=dec_blk0,
                                   init=state_ssd_h[:, j])
            ssd_states.append(hst)
            ssq = jnp.transpose(jnp.concatenate([ssq_c[:, :n_ctx, 0], ssq_d[:, n_ctx:, 0]], axis=1))
            w_out = ssd_w_out[j]
        y = jnp.concatenate([y_c[:n_ctx], y_d[n_ctx:]], axis=0)
        x, h = _out_ln(y, w_out, x, modv, lnv, rows, layer=i, k_gate=5, ln_idx=3 * i + 1,
                       alpha=alpha, coef=1.0, nxt=(i, 6, 7), ssq=ssq,
                       ssq_dim=None if ssq is None else w_out.shape[0], name="mix_out")
        act = _ffn_in(h, ffn_w_in[i, 1])
        nxt = (i + 1, 0, 1) if i + 1 < depth else None
        x, h = _out_ln(act, ffn_w_out[i, 1], x, modv, lnv, rows, layer=i, k_gate=8, ln_idx=3 * i + 2,
                       alpha=alpha, coef=0.5, nxt=nxt, name="ffn_out")

    y_prompt = x[:n_ctx].reshape(bp, sp, d)
    y_sample = x[n_ctx:].reshape(bd, sd, d)
    new_k = jnp.stack([kv[0] for kv in attn_kv], axis=1)
    new_v = jnp.stack([kv[1] for kv in attn_kv], axis=1)
    new_c = jnp.stack([s[0] for s in ml_states], axis=1)
    new_n = jnp.stack([s[1] for s in ml_states], axis=1)
    new_m = jnp.stack([s[2] for s in ml_states], axis=1)
    new_h = jnp.stack(ssd_states, axis=1)
    return (y_prompt, y_sample, new_k, new_v, new_c, new_n, new_m, new_h)
```

```python
import functools
import math

import jax
import jax.numpy as jnp
from jax import lax
from jax.experimental import pallas as pl
from jax.experimental.pallas import tpu as pltpu

F32 = jnp.float32
BF16 = jnp.bfloat16

LN_EPS = 1e-6
N_MOD = 9
N_SEG_PAD = 8
LANES = 128
VMEM_LIMIT = 60 * 1024 * 1024
LN_ROWS = 256

ML_HEADS, ML_DK, ML_DV, ML_CHUNK = 8, 128, 256, 128
ATTN_HEADS, ATTN_KV_HEADS, ATTN_HD = 16, 4, 128
ATTN_GROUP = ATTN_HEADS // ATTN_KV_HEADS
ATTN_QBLOCK = 256
GRID_W = 64
ROPE_THETA = 10000.0
ROPE_PAIRS = ATTN_HD // 4
SSD_HD, SSD_GROUPS, SSD_N, SSD_CONV_W, SSD_CHUNK = 64, 8, 128, 5, 128


def _params(*sem):
    return pltpu.CompilerParams(dimension_semantics=sem, vmem_limit_bytes=VMEM_LIMIT)


def _lead_spec(lead, block, index_fn):
    lead = tuple(lead)
    return pl.BlockSpec((None,) * len(lead) + tuple(block), lambda *g: lead + tuple(index_fn(*g)))


def _call_aliased(kernel_fn, prev, *, in_specs, args, out_specs, out_shape, **kw):
    prev = dict(prev or {})
    n = len(prev)

    def body(*refs):
        return kernel_fn(*refs[n:])

    return pl.pallas_call(
        body,
        in_specs=[pl.BlockSpec(memory_space=pl.ANY)] * n + list(in_specs),
        out_specs=out_specs,
        out_shape=out_shape,
        input_output_aliases={i: o for i, o in enumerate(prev)},
        **kw,
    )(*prev.values(), *args)


def _silu(x):
    return x * jax.nn.sigmoid(x)


def _log_sigmoid(x):
    return jnp.minimum(x, 0.0) - jnp.log1p(jnp.exp(-jnp.abs(x)))


def _softplus(x):
    return jnp.maximum(x, 0.0) + jnp.log1p(jnp.exp(-jnp.abs(x)))


def _adaln_kernel(cv_ref, w_ref, b_ref, o_ref):
    s = _silu(cv_ref[...]).astype(BF16)
    w = w_ref[...].astype(BF16)
    o_ref[...] = jnp.dot(s, w, preferred_element_type=F32) + b_ref[...]


def _adaln(cvec, mod_w, mod_b, bn=1024):
    depth, d, n = mod_w.shape
    return pl.pallas_call(
        _adaln_kernel,
        grid=(depth, n // bn),
        in_specs=[
            pl.BlockSpec((N_SEG_PAD, d), lambda l, j: (0, 0)),
            pl.BlockSpec((None, d, bn), lambda l, j: (l, 0, j)),
            pl.BlockSpec((None, 1, bn), lambda l, j: (l, 0, j)),
        ],
        out_specs=pl.BlockSpec((None, N_SEG_PAD, bn), lambda l, j: (l, 0, j)),
        out_shape=jax.ShapeDtypeStruct((depth, N_SEG_PAD, n), F32),
        compiler_params=_params("arbitrary", "arbitrary"),
        name="adaln",
    )(cvec, mod_w, mod_b.reshape(depth, 1, n))


class _Rows:
    def __init__(self, n_ctx_rows, dec_seq, d_model):
        self.n_ctx = n_ctx_rows
        self.dec_seq = dec_seq
        self.d = d_model

    def seg(self, i, bm):
        r = i * bm
        return jnp.where(r < self.n_ctx, 0, 1 + (r - self.n_ctx) // self.dec_seq)


def _mod_spec(rows, layer, k, bm):
    def idx(i, *_):
        return ((layer * N_SEG_PAD + rows.seg(i, bm)) * N_MOD + k, 0, 0)

    return pl.BlockSpec((None, 1, rows.d), idx)


def _modulate_kernel(x_ref, sh_ref, sc_ref, h_ref):
    h_ref[...] = (x_ref[...] * (1.0 + sc_ref[...]) + sh_ref[...]).astype(BF16)


def _modulate(x, modv, rows, layer, k_shift, k_scale, bm=512):
    m, d = x.shape
    return pl.pallas_call(
        _modulate_kernel,
        grid=(m // bm,),
        in_specs=[
            pl.BlockSpec((bm, d), lambda i: (i, 0)),
            _mod_spec(rows, layer, k_shift, bm),
            _mod_spec(rows, layer, k_scale, bm),
        ],
        out_specs=pl.BlockSpec((bm, d), lambda i: (i, 0)),
        out_shape=jax.ShapeDtypeStruct((m, d), BF16),
        compiler_params=_params("parallel"),
        name="modulate",
    )(x, modv, modv)


def _ffn_in_kernel(a_ref, wg_ref, wu_ref, o_ref, w_scr):
    bn = wg_ref.shape[1]

    @pl.when(pl.program_id(1) == 0)
    def _():
        w_scr[:, :bn] = wg_ref[...].astype(BF16)
        w_scr[:, bn:] = wu_ref[...].astype(BF16)

    u = jnp.dot(a_ref[...], w_scr[...], preferred_element_type=F32)
    o_ref[...] = (_silu(u[:, :bn]) * u[:, bn:]).astype(BF16)


def _ffn_in(a, w_in, lead, bm=1024, bn=512):
    m, k = a.shape
    f = w_in.shape[-1] // 2
    nt = f // bn
    return pl.pallas_call(
        _ffn_in_kernel,
        grid=(nt, m // bm),
        in_specs=[
            pl.BlockSpec((bm, k), lambda j, i: (i, 0)),
            _lead_spec(lead, (k, bn), lambda j, i: (0, j)),
            _lead_spec(lead, (k, bn), lambda j, i: (0, j + nt)),
        ],
        out_specs=pl.BlockSpec((bm, bn), lambda j, i: (i, j)),
        out_shape=jax.ShapeDtypeStruct((m, f), BF16),
        scratch_shapes=[pltpu.VMEM((k, 2 * bn), BF16)],
        compiler_params=_params("arbitrary", "arbitrary"),
        name="ffn_in",
    )(a, w_in, w_in)


def _proj_kernel(a_ref, w_ref, o_ref, w_scr, *, scale_first):
    @pl.when(pl.program_id(1) == 0)
    def _():
        w_scr[...] = w_ref[...].astype(BF16)

    u = jnp.dot(a_ref[...], w_scr[...], preferred_element_type=F32)
    if scale_first is not None:
        u = u * jnp.where(pl.program_id(0) == 0, scale_first, 1.0)
    o_ref[...] = u.astype(o_ref.dtype)


def _proj(a, w, lead, col0, ncols, out_dtype, bm=1024, bn=1024, scale_first=None, name="proj"):
    m, k = a.shape
    assert col0 % bn == 0 and ncols % bn == 0
    j0 = col0 // bn
    return pl.pallas_call(
        functools.partial(_proj_kernel, scale_first=scale_first),
        grid=(ncols // bn, m // bm),
        in_specs=[
            pl.BlockSpec((bm, k), lambda j, i: (i, 0)),
            _lead_spec(lead, (k, bn), lambda j, i: (0, j + j0)),
        ],
        out_specs=pl.BlockSpec((bm, bn), lambda j, i: (i, j)),
        out_shape=jax.ShapeDtypeStruct((m, ncols), out_dtype),
        scratch_shapes=[pltpu.VMEM((k, bn), BF16)],
        compiler_params=_params("arbitrary", "arbitrary"),
        name=name,
    )(a, w)


def _small_proj_kernel(a_ref, w_ref, o_ref):
    o_ref[...] = jnp.dot(a_ref[...], w_ref[...].astype(BF16), preferred_element_type=F32)


def _small_proj(a, w, lead, col_block, ncols, bm=1024, name="small_proj"):
    m, k = a.shape
    return pl.pallas_call(
        _small_proj_kernel,
        grid=(m // bm,),
        in_specs=[
            pl.BlockSpec((bm, k), lambda i: (i, 0)),
            _lead_spec(lead, (k, ncols), lambda i: (0, col_block)),
        ],
        out_specs=pl.BlockSpec((bm, ncols), lambda i: (i, 0)),
        out_shape=jax.ShapeDtypeStruct((m, ncols), F32),
        compiler_params=_params("parallel"),
        name=name,
    )(a, w)


def _out_ln_kernel(*refs, alpha, coef, has_next, has_ssq, ssq_dim):
    it = iter(refs)
    a_ref, w_ref, x_ref, gate_ref, g_ref, b_ref = (next(it) for _ in range(6))
    sh_ref = sc_ref = ssq_ref = None
    if has_next:
        sh_ref, sc_ref = next(it), next(it)
    if has_ssq:
        ssq_ref = next(it)
    xo_ref = next(it)
    ho_ref = next(it) if has_next else None

    k = pl.program_id(1)

    @pl.when(k == 0)
    def _():
        xo_ref[...] = jnp.dot(a_ref[...], w_ref[...].astype(BF16), preferred_element_type=F32)

    @pl.when(k != 0)
    def _():
        xo_ref[...] += jnp.dot(a_ref[...], w_ref[...].astype(BF16), preferred_element_type=F32)

    @pl.when(k == pl.num_programs(1) - 1)
    def _():
        cg = coef * gate_ref[...]

        def slab(si, carry):
            rs = pl.ds(pl.multiple_of(si * LN_ROWS, LN_ROWS), LN_ROWS)
            y = xo_ref[rs, :]
            if has_ssq:
                ms = jnp.sum(ssq_ref[rs, :], axis=1, keepdims=True) * (1.0 / ssq_dim)
                y = y * lax.rsqrt(ms + LN_EPS)
            z = alpha * x_ref[rs, :] + cg * y
            mu = jnp.mean(z, axis=1, keepdims=True)
            zc = z - mu
            var = jnp.mean(zc * zc, axis=1, keepdims=True)
            xn = zc * lax.rsqrt(var + LN_EPS) * g_ref[...] + b_ref[...]
            xo_ref[rs, :] = xn
            if has_next:
                ho_ref[rs, :] = (xn * (1.0 + sc_ref[...]) + sh_ref[...]).astype(BF16)
            return carry

        lax.fori_loop(0, xo_ref.shape[0] // LN_ROWS, slab, 0)


def _out_ln(a, w, lead, x, modv, lnv, rows, *, layer, k_gate, ln_idx, alpha, coef,
            nxt=None, ssq=None, ssq_dim=None, bm=1024, bk=512, name="out_ln"):
    m, kdim = a.shape
    d = x.shape[1]
    assert kdim % bk == 0
    in_specs = [
        pl.BlockSpec((bm, bk), lambda i, k: (i, k)),
        _lead_spec(lead, (bk, d), lambda i, k: (k, 0)),
        pl.BlockSpec((bm, d), lambda i, k: (i, 0)),
        _mod_spec(rows, layer, k_gate, bm),
        pl.BlockSpec((None, 1, d), lambda i, k: (2 * ln_idx, 0, 0)),
        pl.BlockSpec((None, 1, d), lambda i, k: (2 * ln_idx + 1, 0, 0)),
    ]
    args = [a, w, x, modv, lnv, lnv]
    if nxt is not None:
        in_specs += [_mod_spec(rows, nxt[0], nxt[1], bm), _mod_spec(rows, nxt[0], nxt[2], bm)]
        args += [modv, modv]
    if ssq is not None:
        in_specs.append(pl.BlockSpec((bm, ssq.shape[1]), lambda i, k: (i, 0)))
        args.append(ssq)
    out_specs = [pl.BlockSpec((bm, d), lambda i, k: (i, 0))]
    out_shape = [jax.ShapeDtypeStruct((m, d), F32)]
    if nxt is not None:
        out_specs.append(pl.BlockSpec((bm, d), lambda i, k: (i, 0)))
        out_shape.append(jax.ShapeDtypeStruct((m, d), BF16))
    res = pl.pallas_call(
        functools.partial(_out_ln_kernel, alpha=alpha, coef=coef, has_next=nxt is not None,
                          has_ssq=ssq is not None, ssq_dim=ssq_dim),
        grid=(m // bm, kdim // bk),
        in_specs=in_specs,
        out_specs=out_specs,
        out_shape=out_shape,
        compiler_params=_params("parallel", "arbitrary"),
        name=name,
    )(*args)
    return (res[0], res[1]) if nxt is not None else (res[0], None)


def _mlstm_kernel(*refs, n_chunks, has_init, has_state_out):
    it = iter(refs)
    q_ref, k_ref, v_ref, o_ref, gc_ref, gr_ref, bc_ref, br_ref, ng_ref = (next(it) for _ in range(9))
    c0_ref = n0_ref = m0_ref = None
    if has_init:
        c0_ref, n0_ref, m0_ref = next(it), next(it), next(it)
    y_ref = next(it)
    co_ref = no_ref = mo_ref = None
    if has_state_out:
        co_ref, no_ref, mo_ref = next(it), next(it), next(it)
    hs_scr, c_scr, n_scr, m_scr = next(it), next(it), next(it), next(it)

    L = ML_CHUNK
    t_idx = lax.broadcasted_iota(jnp.int32, (L, L), 0)
    s_idx = lax.broadcasted_iota(jnp.int32, (L, L), 1)

    for d in range(2):
        incl = (s_idx <= t_idx) if d == 0 else (s_idx >= t_idx)
        incl_t = (t_idx <= s_idx) if d == 0 else (t_idx >= s_idx)
        if has_init:
            c_scr[...] = c0_ref[d]
            n_scr[...] = n0_ref[d]
            m_scr[...] = m0_ref[d]
        else:
            c_scr[...] = jnp.zeros_like(c_scr)
            n_scr[...] = jnp.zeros_like(n_scr)
            m_scr[...] = jnp.zeros_like(m_scr)

        def chunk(ci, carry, d=d, incl=incl, incl_t=incl_t):
            c = ci if d == 0 else n_chunks - 1 - ci
            r0 = pl.multiple_of(c * L, L)
            qc = q_ref[pl.ds(r0, L), :]
            kc = k_ref[pl.ds(r0, L), :]
            vc = v_ref[pl.ds(r0, L), :]
            gcol = gc_ref[pl.ds(r0, L), :] + bc_ref[...]
            grow = gr_ref[c] + br_ref[...]
            li_c = gcol[:, d:d + 1]
            lf_c = _log_sigmoid(gcol[:, 2 + d:3 + d])
            li_r = grow[d:d + 1, :]
            lf_r = _log_sigmoid(grow[2 + d:3 + d, :])
            b_c = jnp.sum(jnp.where(incl, lf_r, 0.0), axis=1, keepdims=True)
            b_r = jnp.sum(jnp.where(incl_t, lf_c, 0.0), axis=0, keepdims=True)
            g = jnp.sum(lf_r, axis=1, keepdims=True)
            m_prev = m_scr[...]
            dmat = jnp.where(incl, b_c - b_r + li_r, -jnp.inf)
            inter = b_c + m_prev
            m_t = jnp.maximum(inter, jnp.max(dmat, axis=1, keepdims=True))
            w = jnp.exp(dmat - m_t)
            qk = lax.dot_general(qc, kc, (((1,), (1,)), ((), ())), preferred_element_type=F32)
            s = qk * w
            sc = jnp.exp(inter - m_t)
            q_c = jnp.dot(qc, c_scr[...].astype(BF16), preferred_element_type=F32)
            num = sc * q_c + jnp.dot(s.astype(BF16), vc, preferred_element_type=F32)
            qn = jnp.sum(qc.astype(F32) * n_scr[...], axis=1, keepdims=True)
            den = sc * qn + jnp.sum(s, axis=1, keepdims=True)
            h = num / jnp.maximum(jnp.abs(den), jnp.exp(-m_t))
            if d == 0:
                hs_scr[pl.ds(r0, L), :] = h
            else:
                hs_scr[pl.ds(r0, L), :] += h
            ds_c = g - b_c + li_c
            ds_r = g - b_r + li_r
            m_new = jnp.maximum(g + m_prev, jnp.max(ds_r, axis=1, keepdims=True))
            wk = jnp.exp(ds_c - m_new) * kc.astype(F32)
            decay = jnp.exp(g + m_prev - m_new)
            c_scr[...] = decay * c_scr[...] + lax.dot_general(
                wk.astype(BF16), vc, (((0,), (0,)), ((), ())), preferred_element_type=F32)
            n_scr[...] = decay * n_scr[...] + jnp.sum(wk, axis=0, keepdims=True)
            m_scr[...] = m_new
            return carry

        lax.fori_loop(0, n_chunks, chunk, 0)
        if has_state_out:
            co_ref[d] = c_scr[...]
            no_ref[d] = n_scr[...]
            mo_ref[d] = m_scr[...]

    hs = hs_scr[...]
    mu = jnp.mean(hs, axis=1, keepdims=True)
    hc = hs - mu
    var = jnp.mean(hc * hc, axis=1, keepdims=True)
    hn = hc * lax.rsqrt(var + LN_EPS) * ng_ref[...]
    y_ref[...] = (jax.nn.sigmoid(o_ref[...].astype(F32)) * hn).astype(BF16)


def _mlstm_core(qkv, o, gates, b_gate, norm_g, *, batch, seq, row_blk0, init=None, state_out=False, y_prev=None):
    m = qkv.shape[0]
    h, dk, dv, L = ML_HEADS, ML_DK, ML_DV, ML_CHUNK
    nc = seq // L
    r0 = row_blk0 * seq
    g4 = gates[r0:r0 + batch * seq].reshape(batch, seq, 4, h)
    gcol = jnp.transpose(g4, (0, 3, 1, 2))
    grow = jnp.transpose(g4.reshape(batch, nc, L, 4, h), (0, 4, 1, 3, 2))
    b4 = b_gate.reshape(4, h)
    bcol = jnp.transpose(b4)[:, None, :]
    brow = jnp.transpose(b4)[:, :, None]
    kq = (h * dk) // dk
    kv = (2 * h * dk) // dv
    in_specs = [
        pl.BlockSpec((seq, dk), lambda b, hh: (row_blk0 + b, hh)),
        pl.BlockSpec((seq, dk), lambda b, hh: (row_blk0 + b, kq + hh)),
        pl.BlockSpec((seq, dv), lambda b, hh: (row_blk0 + b, kv + hh)),
        pl.BlockSpec((seq, dv), lambda b, hh: (row_blk0 + b, hh)),
        pl.BlockSpec((None, None, seq, 4), lambda b, hh: (b, hh, 0, 0)),
        pl.BlockSpec((None, None, nc, 4, L), lambda b, hh: (b, hh, 0, 0, 0)),
        pl.BlockSpec((None, 1, 4), lambda b, hh: (hh, 0, 0)),
        pl.BlockSpec((None, 4, 1), lambda b, hh: (hh, 0, 0)),
        pl.BlockSpec((1, dv), lambda b, hh: (0, hh)),
    ]
    args = [qkv, qkv, qkv, o, gcol, grow, bcol, brow, norm_g.reshape(1, h * dv)]
    if init is not None:
        c0, n0, m0, lyr = init
        nl = c0.shape[1]
        in_specs += [
            pl.BlockSpec((None, None, 2, None, dk, dv), lambda b, hh: (b, lyr, 0, hh, 0, 0)),
            pl.BlockSpec((None, None, 2, None, 1, dk), lambda b, hh: (b, lyr, 0, hh, 0, 0)),
            pl.BlockSpec((None, None, 2, None, 1, 1), lambda b, hh: (b, lyr, 0, hh, 0, 0)),
        ]
        args += [c0, n0.reshape(batch, nl, 2, h, 1, dk), m0.reshape(batch, nl, 2, h, 1, 1)]
    out_specs = [pl.BlockSpec((seq, dv), lambda b, hh: (row_blk0 + b, hh))]
    out_shape = [jax.ShapeDtypeStruct((m, h * dv), BF16)]
    if state_out:
        out_specs += [
            pl.BlockSpec((None, 2, None, dk, dv), lambda b, hh: (b, 0, hh, 0, 0)),
            pl.BlockSpec((None, 2, None, 1, dk), lambda b, hh: (b, 0, hh, 0, 0)),
            pl.BlockSpec((None, 2, None, 1, 1), lambda b, hh: (b, 0, hh, 0, 0)),
        ]
        out_shape += [
            jax.ShapeDtypeStruct((batch, 2, h, dk, dv), F32),
            jax.ShapeDtypeStruct((batch, 2, h, 1, dk), F32),
            jax.ShapeDtypeStruct((batch, 2, h, 1, 1), F32),
        ]
    return _call_aliased(
        functools.partial(_mlstm_kernel, n_chunks=nc, has_init=init is not None, has_state_out=state_out),
        {} if y_prev is None else {0: y_prev},
        grid=(batch, h),
        in_specs=in_specs,
        args=args,
        out_specs=out_specs,
        out_shape=out_shape,
        scratch_shapes=[
            pltpu.VMEM((seq, dv), F32),
            pltpu.VMEM((dk, dv), F32),
            pltpu.VMEM((1, dk), F32),
            pltpu.VMEM((1, 1), F32),
        ],
        compiler_params=_params("parallel", "parallel"),
        name="mlstm_core",
    )


def _rope(x, cos_t, sin_t):
    lane = lax.broadcasted_iota(jnp.int32, x.shape, 1)
    first = (lane % (2 * ROPE_PAIRS)) < ROPE_PAIRS
    rot = jnp.where(first, pltpu.roll(x, x.shape[1] - ROPE_PAIRS, 1), pltpu.roll(x, ROPE_PAIRS, 1))
    return x * cos_t + rot * sin_t


def _rms(x, g):
    return x * lax.rsqrt(jnp.mean(x * x, axis=1, keepdims=True) + LN_EPS) * g


def _attn_kernel(*refs, seq, has_ctx):
    it = iter(refs)
    q_ref, k_ref, v_ref, qg_ref, kg_ref = (next(it) for _ in range(5))
    cos_ref = sin_ref = ck_ref = cv_ref = None
    if has_ctx:
        cos_ref, sin_ref, ck_ref, cv_ref = (next(it) for _ in range(4))
    o_ref = next(it)
    ko_ref = vo_ref = None
    if not has_ctx:
        ko_ref, vo_ref = next(it), next(it)
    keys_scr, vals_scr = next(it), next(it)

    hd = ATTN_HD
    kn = _rms(k_ref[...].astype(F32), kg_ref[...])
    if has_ctx:
        keys_scr[:seq, :] = _rope(kn, cos_ref[...], sin_ref[...]).astype(BF16)
        keys_scr[seq:, :] = ck_ref[...].astype(BF16)
        vals_scr[:seq, :] = v_ref[...]
        vals_scr[seq:, :] = cv_ref[...].astype(BF16)
    else:
        ko_ref[...] = kn
        vo_ref[...] = v_ref[...].astype(F32)
        keys_scr[...] = kn.astype(BF16)
        vals_scr[...] = v_ref[...]

    qb = min(ATTN_QBLOCK, seq)
    scale = hd ** -0.5

    def qblock(bi, carry):
        r0 = pl.multiple_of(bi * qb, qb)
        for g in range(ATTN_GROUP):
            q = _rms(q_ref[pl.ds(r0, qb), g * hd:(g + 1) * hd].astype(F32), qg_ref[...])
            if has_ctx:
                q = _rope(q, cos_ref[pl.ds(r0, qb), :], sin_ref[pl.ds(r0, qb), :])
            q = (q * scale).astype(BF16)
            s = lax.dot_general(q, keys_scr[...], (((1,), (1,)), ((), ())), preferred_element_type=F32)
            mx = jnp.max(s, axis=1, keepdims=True)
            p = jnp.exp(s - mx)
            den = jnp.sum(p, axis=1, keepdims=True)
            o = jnp.dot(p.astype(BF16), vals_scr[...], preferred_element_type=F32) / den
            o_ref[pl.ds(r0, qb), g * hd:(g + 1) * hd] = o.astype(BF16)
        return carry

    lax.fori_loop(0, seq // qb, qblock, 0)


def _attn_core(qkv, q_g, k_g, *, batch, seq, row_blk0, rope=None, ctx=None, y_prev=None):
    m = qkv.shape[0]
    hd, kvh, grp = ATTN_HD, ATTN_KV_HEADS, ATTN_GROUP
    has_ctx = ctx is not None
    k_off = ATTN_HEADS
    v_off = ATTN_HEADS + kvh
    in_specs = [
        pl.BlockSpec((seq, grp * hd), lambda b, kv: (row_blk0 + b, kv)),
        pl.BlockSpec((seq, hd), lambda b, kv: (row_blk0 + b, k_off + kv)),
        pl.BlockSpec((seq, hd), lambda b, kv: (row_blk0 + b, v_off + kv)),
        pl.BlockSpec((1, hd), lambda b, kv: (0, 0)),
        pl.BlockSpec((1, hd), lambda b, kv: (0, 0)),
    ]
    args = [qkv, qkv, qkv, q_g.reshape(1, hd), k_g.reshape(1, hd)]
    n_keys = seq
    if has_ctx:
        cos_t, sin_t = rope
        ck, cv = ctx
        past = ck.shape[1]
        n_keys = seq + past
        in_specs += [
            pl.BlockSpec((seq, hd), lambda b, kv: (0, 0)),
            pl.BlockSpec((seq, hd), lambda b, kv: (0, 0)),
            pl.BlockSpec((None, past, hd), lambda b, kv: (b, 0, kv)),
            pl.BlockSpec((None, past, hd), lambda b, kv: (b, 0, kv)),
        ]
        args += [cos_t, sin_t, ck, cv]
    out_specs = [pl.BlockSpec((seq, grp * hd), lambda b, kv: (row_blk0 + b, kv))]
    out_shape = [jax.ShapeDtypeStruct((m, ATTN_HEADS * hd), BF16)]
    if not has_ctx:
        out_specs += [pl.BlockSpec((None, seq, hd), lambda b, kv: (b, 0, kv))] * 2
        out_shape += [jax.ShapeDtypeStruct((batch, seq, kvh * hd), F32)] * 2
    return _call_aliased(
        functools.partial(_attn_kernel, seq=seq, has_ctx=has_ctx),
        {} if y_prev is None else {0: y_prev},
        grid=(batch, kvh),
        in_specs=in_specs,
        args=args,
        out_specs=out_specs,
        out_shape=out_shape,
        scratch_shapes=[pltpu.VMEM((n_keys, hd), BF16), pltpu.VMEM((n_keys, hd), BF16)],
        compiler_params=_params("parallel", "parallel"),
        name="attn_core",
    )


def _rope_tables(seq):
    rows = seq // GRID_W
    row = jnp.repeat(jnp.arange(rows, dtype=F32), GRID_W)
    col = jnp.tile(jnp.arange(GRID_W, dtype=F32), rows)
    freqs = ROPE_THETA ** (-jnp.arange(ROPE_PAIRS, dtype=F32) / ROPE_PAIRS)
    ar, ac = row[:, None] * freqs, col[:, None] * freqs
    cos_t = jnp.concatenate([jnp.cos(ar), jnp.cos(ar), jnp.cos(ac), jnp.cos(ac)], axis=1)
    sin_t = jnp.concatenate([-jnp.sin(ar), jnp.sin(ar), -jnp.sin(ac), jnp.sin(ac)], axis=1)
    return cos_t, sin_t


def _dwconv_silu(x, w, b):
    t = x.shape[0]
    row = lax.broadcasted_iota(jnp.int32, x.shape, 0)
    pad = SSD_CONV_W // 2
    acc = jnp.zeros_like(x) + b
    for j in range(SSD_CONV_W):
        off = j - pad
        if off == 0:
            xs = x
        else:
            xs = pltpu.roll(x, (-off) % t, 0)
            valid = (row + off >= 0) & (row + off < t)
            xs = jnp.where(valid, xs, 0.0)
        acc = acc + xs * w[j:j + 1, :]
    return _silu(acc)


def _ssd_kernel(*refs, n_chunks, has_init, has_state_out):
    it = iter(refs)
    (z_ref, x_ref, b_ref, c_ref, wx_ref, wb_ref, wc_ref, bx_ref, bb_ref, bc_ref,
     dtc_ref, dtr_ref, dbc_ref, dbr_ref, alc_ref, alr_ref, dsk_ref, ng_ref) = (next(it) for _ in range(18))
    h0_ref = next(it) if has_init else None
    y_ref, ssq_ref = next(it), next(it)
    ho_ref = next(it) if has_state_out else None
    xs_scr, bm_scr, cm_scr, y_scr, h_scr = (next(it) for _ in range(5))

    L, P, E = SSD_CHUNK, SSD_HD, x_ref.shape[1] // SSD_HD

    xs_scr[...] = _dwconv_silu(x_ref[...].astype(F32), wx_ref[...], bx_ref[...])
    bm_scr[...] = _dwconv_silu(b_ref[...].astype(F32), wb_ref[...], bb_ref[...]).astype(BF16)
    cm_scr[...] = _dwconv_silu(c_ref[...].astype(F32), wc_ref[...], bc_ref[...]).astype(BF16)

    t_idx = lax.broadcasted_iota(jnp.int32, (L, L), 0)
    s_idx = lax.broadcasted_iota(jnp.int32, (L, L), 1)

    for d in range(2):
        incl = (s_idx <= t_idx) if d == 0 else (s_idx >= t_idx)
        incl_t = (t_idx <= s_idx) if d == 0 else (t_idx >= s_idx)
        a_c = -jnp.exp(alc_ref[...][:, d * E:(d + 1) * E])
        a_r = -jnp.exp(alr_ref[...][d * E:(d + 1) * E, :])
        if has_init:
            h_scr[...] = h0_ref[d]
        else:
            h_scr[...] = jnp.zeros_like(h_scr)

        def chunk(ci, carry, d=d, incl=incl, incl_t=incl_t, a_c=a_c, a_r=a_r):
            c = ci if d == 0 else n_chunks - 1 - ci
            r0 = pl.multiple_of(c * L, L)
            bc = bm_scr[pl.ds(r0, L), :]
            cc = cm_scr[pl.ds(r0, L), :]
            dt_c = _softplus(dtc_ref[pl.ds(r0, L), :][:, d * E:(d + 1) * E]
                             + dbc_ref[...][:, d * E:(d + 1) * E])
            dt_r = _softplus(dtr_ref[c][d * E:(d + 1) * E, :]
                             + dbr_ref[...][d * E:(d + 1) * E, :])
            da_c = dt_c * a_c
            da_r = dt_r * a_r
            cb = lax.dot_general(cc, bc, (((1,), (1,)), ((), ())), preferred_element_type=F32)
            ys = []
            for e in range(E):
                cs_c = jnp.sum(jnp.where(incl, da_r[e:e + 1, :], 0.0), axis=1, keepdims=True)
                cs_r = jnp.sum(jnp.where(incl_t, da_c[:, e:e + 1], 0.0), axis=0, keepdims=True)
                tot = jnp.sum(da_r[e:e + 1, :], axis=1, keepdims=True)
                decay = jnp.exp(jnp.where(incl, cs_c - cs_r, -jnp.inf))
                xdt = xs_scr[pl.ds(r0, L), e * P:(e + 1) * P] * dt_c[:, e:e + 1]
                y_d = jnp.dot((cb * decay).astype(BF16), xdt.astype(BF16), preferred_element_type=F32)
                h_e = h_scr[e]
                y_o = lax.dot_general(cc, h_e.astype(BF16), (((1,), (1,)), ((), ())),
                                      preferred_element_type=F32) * jnp.exp(cs_c)
                ys.append(y_d + y_o)
                tail = jnp.exp(tot - cs_c)
                h_scr[e] = jnp.exp(tot) * h_e + lax.dot_general(
                    (tail * xdt).astype(BF16), bc, (((0,), (0,)), ((), ())), preferred_element_type=F32)
            y = jnp.concatenate(ys, axis=1)
            if d == 0:
                y_scr[pl.ds(r0, L), :] = y
            else:
                y_scr[pl.ds(r0, L), :] += y
            return carry

        lax.fori_loop(0, n_chunks, chunk, 0)
        if has_state_out:
            ho_ref[d] = h_scr[...]

    y = y_scr[...] + dsk_ref[...] * xs_scr[...]
    yz = y * _silu(z_ref[...].astype(F32))
    ssq_ref[...] = jnp.sum(yz * yz, axis=1, keepdims=True)
    y_ref[...] = (yz * ng_ref[...]).astype(BF16)


def _ssd_core(zxbc, dt_raw, conv_w, conv_b, dt_bias, a_log, d_skip, norm_g, *,
              batch, seq, row_blk0, init=None, state_out=False, prev=None):
    m = zxbc.shape[0]
    G, P, N, L = SSD_GROUPS, SSD_HD, SSD_N, SSD_CHUNK
    di = d_skip.shape[0] * P
    heads = di // P
    E = heads // G
    gw = E * P
    nc = seq // L
    r0 = row_blk0 * seq
    dt4 = dt_raw[r0:r0 + batch * seq].reshape(batch, seq, 2, G, E)
    dtc = jnp.transpose(dt4, (0, 3, 1, 2, 4)).reshape(batch, G, seq, 2 * E)
    dtr = jnp.transpose(dt4.reshape(batch, nc, L, 2, G, E), (0, 4, 1, 3, 5, 2)).reshape(batch, G, nc, 2 * E, L)

    def per_group(v):
        v3 = jnp.transpose(v.reshape(2, G, E), (1, 0, 2)).reshape(G, 2 * E)
        return v3[:, None, :], v3[:, :, None]

    dbc, dbr = per_group(dt_bias)
    alc, alr = per_group(a_log)
    x_blk0 = di // gw
    b_blk0 = (2 * di) // N
    c_blk0 = (2 * di + G * N) // N
    in_specs = [
        pl.BlockSpec((seq, gw), lambda b, g: (row_blk0 + b, g)),
        pl.BlockSpec((seq, gw), lambda b, g: (row_blk0 + b, x_blk0 + g)),
        pl.BlockSpec((seq, N), lambda b, g: (row_blk0 + b, b_blk0 + g)),
        pl.BlockSpec((seq, N), lambda b, g: (row_blk0 + b, c_blk0 + g)),
        pl.BlockSpec((SSD_CONV_W, gw), lambda b, g: (0, g)),
        pl.BlockSpec((SSD_CONV_W, N), lambda b, g: (0, di // N + g)),
        pl.BlockSpec((SSD_CONV_W, N), lambda b, g: (0, di // N + G + g)),
        pl.BlockSpec((1, gw), lambda b, g: (0, g)),
        pl.BlockSpec((1, N), lambda b, g: (0, di // N + g)),
        pl.BlockSpec((1, N), lambda b, g: (0, di // N + G + g)),
        pl.BlockSpec((None, None, seq, 2 * E), lambda b, g: (b, g, 0, 0)),
        pl.BlockSpec((None, None, nc, 2 * E, L), lambda b, g: (b, g, 0, 0, 0)),
        pl.BlockSpec((None, 1, 2 * E), lambda b, g: (g, 0, 0)),
        pl.BlockSpec((None, 2 * E, 1), lambda b, g: (g, 0, 0)),
        pl.BlockSpec((None, 1, 2 * E), lambda b, g: (g, 0, 0)),
        pl.BlockSpec((None, 2 * E, 1), lambda b, g: (g, 0, 0)),
        pl.BlockSpec((1, gw), lambda b, g: (0, g)),
        pl.BlockSpec((1, gw), lambda b, g: (0, g)),
    ]
    cb2 = conv_b.reshape(1, -1)
    args = [zxbc, zxbc, zxbc, zxbc, conv_w, conv_w, conv_w, cb2, cb2, cb2,
            dtc, dtr, dbc, dbr, alc, alr,
            jnp.repeat(d_skip, P).reshape(1, di), norm_g.reshape(1, di)]
    if init is not None:
        in_specs.append(pl.BlockSpec((None, 2, E, P, N), lambda b, g: (b, 0, g, 0, 0)))
        args.append(init)
    out_specs = [
        pl.BlockSpec((seq, gw), lambda b, g: (row_blk0 + b, g)),
        pl.BlockSpec((None, seq, 1), lambda b, g: (g, row_blk0 + b, 0)),
    ]
    out_shape = [jax.ShapeDtypeStruct((m, di), BF16), jax.ShapeDtypeStruct((G, m, 1), F32)]
    if state_out:
        out_specs.append(pl.BlockSpec((None, 2, E, P, N), lambda b, g: (b, 0, g, 0, 0)))
        out_shape.append(jax.ShapeDtypeStruct((batch, 2, heads, P, N), F32))
    return _call_aliased(
        functools.partial(_ssd_kernel, n_chunks=nc, has_init=init is not None, has_state_out=state_out),
        {} if prev is None else {0: prev[0], 1: prev[1]},
        grid=(batch, G),
        in_specs=in_specs,
        args=args,
        out_specs=out_specs,
        out_shape=out_shape,
        scratch_shapes=[
            pltpu.VMEM((seq, gw), F32),
            pltpu.VMEM((seq, N), BF16),
            pltpu.VMEM((seq, N), BF16),
            pltpu.VMEM((seq, gw), F32),
            pltpu.VMEM((E, P, N), F32),
        ],
        compiler_params=_params("parallel", "parallel"),
        name="ssd_core",
    )


def kernel(x_prompt, x_sample, cache_attn_k, cache_attn_v, state_mlstm_C, state_mlstm_n, state_mlstm_m, state_ssd_h, c, c_ctx, mod_w, mod_b, ln_g, ln_b, ffn_w_in, ffn_w_out, mlstm_w_in, mlstm_b_gate, mlstm_norm_g, mlstm_w_out, attn_w_qkv, attn_q_norm, attn_k_norm, attn_w_out, ssd_w_in, ssd_conv_w, ssd_conv_b, ssd_dt_bias, ssd_A_log, ssd_D, ssd_norm_g, ssd_w_out):
    bp, sp, d = x_prompt.shape
    bd, sd, _ = x_sample.shape
    depth = mod_w.shape[0]
    n_ctx = bp * sp
    m = n_ctx + bd * sd
    rows = _Rows(n_ctx, sd, d)
    alpha = (2 * depth) ** 0.25
    ctx_blk0, dec_blk0 = 0, n_ctx // sd

    cvec = jnp.zeros((N_SEG_PAD, d), F32).at[0].set(c_ctx).at[1:1 + bd].set(c)
    modv = _adaln(cvec, mod_w, mod_b).reshape(depth * N_SEG_PAD * N_MOD, 1, d)
    lnv = jnp.stack([ln_g, ln_b], axis=2).reshape(depth * 3 * 2, 1, d)

    x = jnp.concatenate([x_prompt.reshape(n_ctx, d), x_sample.reshape(bd * sd, d)], axis=0)
    h = _modulate(x, modv, rows, 0, 0, 1)

    qk_w, v_w = 2 * ML_HEADS * ML_DK, ML_HEADS * ML_DV
    mlstm_w_gate = mlstm_w_in[:, :, qk_w + 2 * v_w:]

    ml_states, attn_kv, ssd_states = [], [], []
    for i in range(depth):
        kind, j = i % 3, i // 3
        act = _ffn_in(h, ffn_w_in, (i, 0))
        x, h = _out_ln(act, ffn_w_out, (i, 0), x, modv, lnv, rows, layer=i, k_gate=2, ln_idx=3 * i,
                       alpha=alpha, coef=0.5, nxt=(i, 3, 4), name="ffn_out")
        ssq = None
        if kind == 0:
            qkv = _proj(h, mlstm_w_in, (j,), 0, qk_w + v_w, BF16, scale_first=ML_DK ** -0.5,
                        bn=ML_HEADS * ML_DK, name="mlstm_qkv")
            og = _proj(h, mlstm_w_in, (j,), qk_w + v_w, v_w, BF16, name="mlstm_o")
            gates = _small_proj(h, mlstm_w_gate, (j,), 0, 4 * ML_HEADS, name="mlstm_gates")
            y, cs, ns, ms = _mlstm_core(qkv, og, gates, mlstm_b_gate[j], mlstm_norm_g[j],
                                        batch=bp, seq=sp, row_blk0=ctx_blk0, state_out=True)
            (y,) = _mlstm_core(qkv, og, gates, mlstm_b_gate[j], mlstm_norm_g[j],
                               batch=bd, seq=sd, row_blk0=dec_blk0, y_prev=y,
                               init=(state_mlstm_C, state_mlstm_n, state_mlstm_m, j))
            ml_states.append((cs, ns.reshape(bp, 2, ML_HEADS, ML_DK), ms.reshape(bp, 2, ML_HEADS)))
            w_out = mlstm_w_out
        elif kind == 1:
            qkv = _proj(h, attn_w_qkv, (j,), 0, attn_w_qkv.shape[2], BF16, name="attn_qkv")
            y, kc, vc = _attn_core(qkv, attn_q_norm[j], attn_k_norm[j], batch=bp, seq=sp, row_blk0=ctx_blk0)
            past = cache_attn_k.shape[2]
            (y,) = _attn_core(qkv, attn_q_norm[j], attn_k_norm[j], batch=bd, seq=sd, row_blk0=dec_blk0,
                              rope=_rope_tables(sd), y_prev=y,
                              ctx=(cache_attn_k[:, j].reshape(bd, past, -1), cache_attn_v[:, j].reshape(bd, past, -1)))
            attn_kv.append((kc.reshape(bp, sp, ATTN_KV_HEADS, ATTN_HD), vc.reshape(bp, sp, ATTN_KV_HEADS, ATTN_HD)))
            w_out = attn_w_out
        else:
            di = ssd_D.shape[1] * SSD_HD
            zxbc_w = 2 * di + 2 * SSD_GROUPS * SSD_N
            zxbc = _proj(h, ssd_w_in, (j,), 0, zxbc_w, BF16, name="ssd_zxbc")
            dt_raw = _small_proj(h, ssd_w_in, (j,), zxbc_w // LANES, LANES, name="ssd_dt")
            y, ssq3, hst = _ssd_core(zxbc, dt_raw, ssd_conv_w[j], ssd_conv_b[j], ssd_dt_bias[j], ssd_A_log[j],
                                     ssd_D[j], ssd_norm_g[j], batch=bp, seq=sp, row_blk0=ctx_blk0, state_out=True)
            y, ssq3 = _ssd_core(zxbc, dt_raw, ssd_conv_w[j], ssd_conv_b[j], ssd_dt_bias[j], ssd_A_log[j],
                                ssd_D[j], ssd_norm_g[j], batch=bd, seq=sd, row_blk0=dec_blk0,
                                init=state_ssd_h[:, j], prev=(y, ssq3))
            ssd_states.append(hst)
            ssq = jnp.transpose(ssq3[:, :, 0])
            w_out = ssd_w_out
        x, h = _out_ln(y, w_out, (j,), x, modv, lnv, rows, layer=i, k_gate=5, ln_idx=3 * i + 1,
                       alpha=alpha, coef=1.0, nxt=(i, 6, 7), ssq=ssq,
                       ssq_dim=None if ssq is None else w_out.shape[1], name="mix_out")
        act = _ffn_in(h, ffn_w_in, (i, 1))
        nxt = (i + 1, 0, 1) if i + 1 < depth else None
        x, h = _out_ln(act, ffn_w_out, (i, 1), x, modv, lnv, rows, layer=i, k_gate=8, ln_idx=3 * i + 2,
                       alpha=alpha, coef=0.5, nxt=nxt, name="ffn_out")

    y_prompt = x[:n_ctx].reshape(bp, sp, d)
    y_sample = x[n_ctx:].reshape(bd, sd, d)
    new_k = jnp.stack([kv[0] for kv in attn_kv], axis=1)
    new_v = jnp.stack([kv[1] for kv in attn_kv], axis=1)
    new_c = jnp.stack([s[0] for s in ml_states], axis=1)
    new_n = jnp.stack([s[1] for s in ml_states], axis=1)
    new_m = jnp.stack([s[2] for s in ml_states], axis=1)
    new_h = jnp.stack(ssd_states, axis=1)
    return (y_prompt, y_sample, new_k, new_v, new_c, new_n, new_m, new_h)
```

```python
import functools
import math

import jax
import jax.numpy as jnp
from jax import lax
from jax.experimental import pallas as pl
from jax.experimental.pallas import tpu as pltpu

F32 = jnp.float32
BF16 = jnp.bfloat16

LN_EPS = 1e-6
N_MOD = 9
N_SEG_PAD = 8
LANES = 128
VMEM_LIMIT = 60 * 1024 * 1024
LN_ROWS = 256

ML_HEADS, ML_DK, ML_DV, ML_CHUNK = 8, 128, 256, 128
ML_HEADS_PER_STEP = 4
ATTN_HEADS, ATTN_KV_HEADS, ATTN_HD = 16, 4, 128
ATTN_GROUP = ATTN_HEADS // ATTN_KV_HEADS
ATTN_QBLOCK = 256
GRID_W = 64
ROPE_THETA = 10000.0
ROPE_PAIRS = ATTN_HD // 4
SSD_HD, SSD_GROUPS, SSD_N, SSD_CONV_W, SSD_CHUNK = 64, 8, 128, 5, 128


def _params(*sem):
    return pltpu.CompilerParams(dimension_semantics=sem, vmem_limit_bytes=VMEM_LIMIT)


def _lead_spec(lead, block, index_fn):
    lead = tuple(lead)
    return pl.BlockSpec((None,) * len(lead) + tuple(block), lambda *g: lead + tuple(index_fn(*g)))


def _call_aliased(kernel_fn, prev, *, in_specs, args, out_specs, out_shape, **kw):
    prev = dict(prev or {})
    n = len(prev)

    def body(*refs):
        return kernel_fn(*refs[n:])

    return pl.pallas_call(
        body,
        in_specs=[pl.BlockSpec(memory_space=pl.ANY)] * n + list(in_specs),
        out_specs=out_specs,
        out_shape=out_shape,
        input_output_aliases={i: o for i, o in enumerate(prev)},
        **kw,
    )(*prev.values(), *args)


def _silu(x):
    return x * jax.nn.sigmoid(x)


def _log_sigmoid(x):
    return jnp.minimum(x, 0.0) - jnp.log1p(jnp.exp(-jnp.abs(x)))


def _softplus(x):
    return jnp.maximum(x, 0.0) + jnp.log1p(jnp.exp(-jnp.abs(x)))


def _split_bf16(x, n):
    parts, r = [], x
    for _ in range(n):
        p = r.astype(BF16)
        parts.append(p)
        r = r - p.astype(F32)
    return parts


def _split3(x):
    return _split_bf16(x, 3)


def _split2(x):
    return _split_bf16(x, 2)


def _dot_split(parts, mask_bf, *, lhs_is_mask):
    acc = None
    for p in reversed(parts):
        t = (jnp.dot(mask_bf, p, preferred_element_type=F32) if lhs_is_mask
             else jnp.dot(p, mask_bf, preferred_element_type=F32))
        acc = t if acc is None else acc + t
    return acc


def _adaln_kernel(cv_ref, w_ref, b_ref, o_ref):
    s = _silu(cv_ref[...]).astype(BF16)
    w = w_ref[...].astype(BF16)
    o_ref[...] = jnp.dot(s, w, preferred_element_type=F32) + b_ref[...]


def _adaln(cvec, mod_w, mod_b, bn=1024):
    depth, d, n = mod_w.shape
    return pl.pallas_call(
        _adaln_kernel,
        grid=(depth, n // bn),
        in_specs=[
            pl.BlockSpec((N_SEG_PAD, d), lambda l, j: (0, 0)),
            pl.BlockSpec((None, d, bn), lambda l, j: (l, 0, j)),
            pl.BlockSpec((None, 1, bn), lambda l, j: (l, 0, j)),
        ],
        out_specs=pl.BlockSpec((None, N_SEG_PAD, bn), lambda l, j: (l, 0, j)),
        out_shape=jax.ShapeDtypeStruct((depth, N_SEG_PAD, n), F32),
        compiler_params=_params("arbitrary", "arbitrary"),
        name="adaln",
    )(cvec, mod_w, mod_b.reshape(depth, 1, n))


class _Rows:
    def __init__(self, n_ctx_rows, dec_seq, d_model):
        self.n_ctx = n_ctx_rows
        self.dec_seq = dec_seq
        self.d = d_model

    def seg(self, i, bm):
        r = i * bm
        return jnp.where(r < self.n_ctx, 0, 1 + (r - self.n_ctx) // self.dec_seq)


def _mod_spec(rows, layer, k, bm):
    def idx(i, *_):
        return ((layer * N_SEG_PAD + rows.seg(i, bm)) * N_MOD + k, 0, 0)

    return pl.BlockSpec((None, 1, rows.d), idx)


def _modulate_kernel(x_ref, sh_ref, sc_ref, h_ref):
    h_ref[...] = (x_ref[...] * (1.0 + sc_ref[...]) + sh_ref[...]).astype(BF16)


def _modulate(x, modv, rows, layer, k_shift, k_scale, bm=512):
    m, d = x.shape
    return pl.pallas_call(
        _modulate_kernel,
        grid=(m // bm,),
        in_specs=[
            pl.BlockSpec((bm, d), lambda i: (i, 0)),
            _mod_spec(rows, layer, k_shift, bm),
            _mod_spec(rows, layer, k_scale, bm),
        ],
        out_specs=pl.BlockSpec((bm, d), lambda i: (i, 0)),
        out_shape=jax.ShapeDtypeStruct((m, d), BF16),
        compiler_params=_params("parallel"),
        name="modulate",
    )(x, modv, modv)


def _ffn_in_kernel(a_ref, wg_ref, wu_ref, o_ref, w_scr):
    bn = wg_ref.shape[1]

    @pl.when(pl.program_id(1) == 0)
    def _():
        w_scr[:, :bn] = wg_ref[...].astype(BF16)
        w_scr[:, bn:] = wu_ref[...].astype(BF16)

    u = jnp.dot(a_ref[...], w_scr[...], preferred_element_type=F32)
    o_ref[...] = (_silu(u[:, :bn]) * u[:, bn:]).astype(BF16)


def _ffn_in(a, w_in, lead, bm=1024, bn=512):
    m, k = a.shape
    f = w_in.shape[-1] // 2
    nt = f // bn
    return pl.pallas_call(
        _ffn_in_kernel,
        grid=(nt, m // bm),
        in_specs=[
            pl.BlockSpec((bm, k), lambda j, i: (i, 0)),
            _lead_spec(lead, (k, bn), lambda j, i: (0, j)),
            _lead_spec(lead, (k, bn), lambda j, i: (0, j + nt)),
        ],
        out_specs=pl.BlockSpec((bm, bn), lambda j, i: (i, j)),
        out_shape=jax.ShapeDtypeStruct((m, f), BF16),
        scratch_shapes=[pltpu.VMEM((k, 2 * bn), BF16)],
        compiler_params=_params("arbitrary", "arbitrary"),
        name="ffn_in",
    )(a, w_in, w_in)


def _proj_kernel(a_ref, w_ref, o_ref, w_scr, *, scale_first, w_t):
    @pl.when(pl.program_id(1) == 0)
    def _():
        w = w_ref[...]
        w_scr[...] = (jnp.transpose(w) if w_t else w).astype(BF16)

    u = jnp.dot(a_ref[...], w_scr[...], preferred_element_type=F32)
    if scale_first is not None:
        u = u * jnp.where(pl.program_id(0) == 0, scale_first, 1.0)
    o_ref[...] = u.astype(o_ref.dtype)


def _proj(a, w, lead, col0, ncols, out_dtype, bm=1024, bn=1024, scale_first=None, w_t=False, name="proj"):
    m, k = a.shape
    assert col0 % bn == 0 and ncols % bn == 0
    j0 = col0 // bn
    w_spec = (_lead_spec(lead, (bn, k), lambda j, i: (j + j0, 0)) if w_t
              else _lead_spec(lead, (k, bn), lambda j, i: (0, j + j0)))
    return pl.pallas_call(
        functools.partial(_proj_kernel, scale_first=scale_first, w_t=w_t),
        grid=(ncols // bn, m // bm),
        in_specs=[
            pl.BlockSpec((bm, k), lambda j, i: (i, 0)),
            w_spec,
        ],
        out_specs=pl.BlockSpec((bm, bn), lambda j, i: (i, j)),
        out_shape=jax.ShapeDtypeStruct((m, ncols), out_dtype),
        scratch_shapes=[pltpu.VMEM((k, bn), BF16)],
        compiler_params=_params("arbitrary", "arbitrary"),
        name=name,
    )(a, w)


def _small_proj_kernel(a_ref, w_ref, o_ref, *, w_t):
    w = w_ref[...].astype(BF16)
    if w_t:
        o_ref[...] = lax.dot_general(a_ref[...], w, (((1,), (1,)), ((), ())), preferred_element_type=F32)
    else:
        o_ref[...] = jnp.dot(a_ref[...], w, preferred_element_type=F32)


def _small_proj(a, w, lead, col_block, ncols, bm=1024, w_t=False, name="small_proj"):
    m, k = a.shape
    w_spec = (_lead_spec(lead, (ncols, k), lambda i: (col_block, 0)) if w_t
              else _lead_spec(lead, (k, ncols), lambda i: (0, col_block)))
    return pl.pallas_call(
        functools.partial(_small_proj_kernel, w_t=w_t),
        grid=(m // bm,),
        in_specs=[
            pl.BlockSpec((bm, k), lambda i: (i, 0)),
            w_spec,
        ],
        out_specs=pl.BlockSpec((bm, ncols), lambda i: (i, 0)),
        out_shape=jax.ShapeDtypeStruct((m, ncols), F32),
        compiler_params=_params("parallel"),
        name=name,
    )(a, w)


def _out_ln_kernel(*refs, alpha, coef, has_next, has_ssq, ssq_dim):
    it = iter(refs)
    a_ref, w_ref, x_ref, gate_ref, g_ref, b_ref = (next(it) for _ in range(6))
    sh_ref = sc_ref = ssq_ref = None
    if has_next:
        sh_ref, sc_ref = next(it), next(it)
    if has_ssq:
        ssq_ref = next(it)
    xo_ref = next(it)
    ho_ref = next(it) if has_next else None

    k = pl.program_id(1)

    @pl.when(k == 0)
    def _():
        xo_ref[...] = jnp.dot(a_ref[...], w_ref[...].astype(BF16), preferred_element_type=F32)

    @pl.when(k != 0)
    def _():
        xo_ref[...] += jnp.dot(a_ref[...], w_ref[...].astype(BF16), preferred_element_type=F32)

    @pl.when(k == pl.num_programs(1) - 1)
    def _():
        cg = coef * gate_ref[...]

        def slab(si, carry):
            rs = pl.ds(pl.multiple_of(si * LN_ROWS, LN_ROWS), LN_ROWS)
            y = xo_ref[rs, :]
            if has_ssq:
                y = y * lax.rsqrt(ssq_ref[rs, :] * (1.0 / ssq_dim) + LN_EPS)
            z = alpha * x_ref[rs, :] + cg * y
            mu = jnp.mean(z, axis=1, keepdims=True)
            zc = z - mu
            var = jnp.mean(zc * zc, axis=1, keepdims=True)
            xn = zc * lax.rsqrt(var + LN_EPS) * g_ref[...] + b_ref[...]
            xo_ref[rs, :] = xn
            if has_next:
                ho_ref[rs, :] = (xn * (1.0 + sc_ref[...]) + sh_ref[...]).astype(BF16)
            return carry

        lax.fori_loop(0, xo_ref.shape[0] // LN_ROWS, slab, 0)


def _out_ln(a, w, lead, x, modv, lnv, rows, *, layer, k_gate, ln_idx, alpha, coef,
            nxt=None, ssq=None, ssq_dim=None, bm=1024, bk=512, name="out_ln"):
    m, kdim = a.shape
    d = x.shape[1]
    assert kdim % bk == 0
    in_specs = [
        pl.BlockSpec((bm, bk), lambda i, k: (i, k)),
        _lead_spec(lead, (bk, d), lambda i, k: (k, 0)),
        pl.BlockSpec((bm, d), lambda i, k: (i, 0)),
        _mod_spec(rows, layer, k_gate, bm),
        pl.BlockSpec((None, 1, d), lambda i, k: (2 * ln_idx, 0, 0)),
        pl.BlockSpec((None, 1, d), lambda i, k: (2 * ln_idx + 1, 0, 0)),
    ]
    args = [a, w, x, modv, lnv, lnv]
    if nxt is not None:
        in_specs += [_mod_spec(rows, nxt[0], nxt[1], bm), _mod_spec(rows, nxt[0], nxt[2], bm)]
        args += [modv, modv]
    if ssq is not None:
        in_specs.append(pl.BlockSpec((bm, ssq.shape[1]), lambda i, k: (i, 0)))
        args.append(ssq)
    out_specs = [pl.BlockSpec((bm, d), lambda i, k: (i, 0))]
    out_shape = [jax.ShapeDtypeStruct((m, d), F32)]
    if nxt is not None:
        out_specs.append(pl.BlockSpec((bm, d), lambda i, k: (i, 0)))
        out_shape.append(jax.ShapeDtypeStruct((m, d), BF16))
    res = pl.pallas_call(
        functools.partial(_out_ln_kernel, alpha=alpha, coef=coef, has_next=nxt is not None,
                          has_ssq=ssq is not None, ssq_dim=ssq_dim),
        grid=(m // bm, kdim // bk),
        in_specs=in_specs,
        out_specs=out_specs,
        out_shape=out_shape,
        compiler_params=_params("parallel", "arbitrary"),
        name=name,
    )(*args)
    return (res[0], res[1]) if nxt is not None else (res[0], None)


def _mlstm_kernel(*refs, n_chunks, hp, has_init, has_state_out):
    it = iter(refs)
    q_ref, k_ref, v_ref, o_ref, gr_ref, br_ref, ng_ref = (next(it) for _ in range(7))
    c0_ref = n0_ref = m0_ref = None
    if has_init:
        c0_ref, n0_ref, m0_ref = next(it), next(it), next(it)
    y_ref = next(it)
    co_ref = no_ref = mo_ref = None
    if has_state_out:
        co_ref, no_ref, mo_ref = next(it), next(it), next(it)
    hf_scr, hb_scr, kt_scr, cn_scr, m_scr = (next(it) for _ in range(5))

    L, dk, dv = ML_CHUNK, ML_DK, ML_DV
    t_idx = lax.broadcasted_iota(jnp.int32, (L, L), 0)
    s_idx = lax.broadcasted_iota(jnp.int32, (L, L), 1)
    ones_bf = jnp.ones((L, LANES), BF16)
    masks = []
    for d in range(2):
        incl = (s_idx <= t_idx) if d == 0 else (s_idx >= t_idx)
        incl_t = (t_idx <= s_idx) if d == 0 else (t_idx >= s_idx)
        masks.append((incl, jnp.where(incl_t, 1.0, 0.0).astype(BF16)))

    for hh in range(hp):
        for c in range(n_chunks):
            kt_scr[c, hh * dk:(hh + 1) * dk, :] = jnp.transpose(
                k_ref[c * L:(c + 1) * L, hh * dk:(hh + 1) * dk].astype(F32))
        for d in range(2):
            i = hh * 2 + d
            if has_init:
                cn_scr[i, :, :dv] = c0_ref[d, hh]
                cn_scr[i, :, dv:] = jnp.transpose(jnp.broadcast_to(n0_ref[d, hh], (LANES, dk)))
                m_scr[i] = m0_ref[d, hh]
            else:
                cn_scr[i] = jnp.zeros((dk, dv + LANES), F32)
                m_scr[i] = jnp.zeros((1, 1), F32)

    def step(ci, carry):
        for d in range(2):
            incl, incl_t_bf = masks[d]
            c = ci if d == 0 else n_chunks - 1 - ci
            r0 = pl.multiple_of(c * L, L)
            last = L - 1 if d == 0 else 0
            grow = gr_ref[c] + br_ref[...]
            lf_parts = _split3(_log_sigmoid(grow))
            cum_r = _dot_split(lf_parts, incl_t_bf, lhs_is_mask=False)
            lf_parts = [p.astype(F32) for p in lf_parts]
            hs_scr = hf_scr if d == 0 else hb_scr
            for hh in range(hp):
                i = hh * 2 + d
                gi, gf = hh * 4 + d, hh * 4 + 2 + d
                qc = q_ref[pl.ds(r0, L), hh * dk:(hh + 1) * dk]
                vc = v_ref[pl.ds(r0, L), hh * dv:(hh + 1) * dv]
                kc = k_ref[pl.ds(r0, L), hh * dk:(hh + 1) * dk]
                li_r, b_r = grow[gi:gi + 1, :], cum_r[gf:gf + 1, :]
                b_c = None
                for p in reversed(lf_parts):
                    t = jnp.dot(jnp.where(incl, p[gf:gf + 1, :], 0.0).astype(BF16), ones_bf,
                                preferred_element_type=F32)
                    b_c = t if b_c is None else b_c + t
                g = b_r[:, last:last + 1]
                m_prev = m_scr[i]
                cn_prev = cn_scr[i]
                dmat = jnp.where(incl, b_c - b_r + li_r, -jnp.inf)
                inter = b_c + m_prev
                m_t = jnp.maximum(inter, jnp.max(dmat, axis=1, keepdims=True))
                w = jnp.exp(dmat - m_t)
                qk = lax.dot_general(qc, kc, (((1,), (1,)), ((), ())), preferred_element_type=F32)
                s_bf = (qk * w).astype(BF16)
                sc = jnp.exp(inter - m_t)
                q_cn = jnp.dot(qc, cn_prev.astype(BF16), preferred_element_type=F32)
                s_v = jnp.dot(s_bf, vc, preferred_element_type=F32)
                den = sc * q_cn[:, dv:] + jnp.dot(s_bf, ones_bf, preferred_element_type=F32)
                inv = 1.0 / jnp.maximum(jnp.abs(den), jnp.exp(-m_t))
                for cb in range(dv // LANES):
                    cs = slice(cb * LANES, (cb + 1) * LANES)
                    hs_scr[pl.ds(r0, L), hh * dv + cb * LANES:hh * dv + (cb + 1) * LANES] = (
                        sc * q_cn[:, cs] + s_v[:, cs]) * inv
                ds_r = g - b_r + li_r
                m_new = jnp.maximum(g + m_prev, jnp.max(ds_r, axis=1, keepdims=True))
                ktw = (kt_scr[c, hh * dk:(hh + 1) * dk, :] * jnp.exp(ds_r - m_new)).astype(BF16)
                decay = jnp.exp(g + m_prev - m_new)
                cn_scr[i, :, :dv] = decay * cn_prev[:, :dv] + jnp.dot(ktw, vc, preferred_element_type=F32)
                cn_scr[i, :, dv:] = decay * cn_prev[:, dv:] + jnp.dot(ktw, ones_bf, preferred_element_type=F32)
                m_scr[i] = m_new
        return carry

    lax.fori_loop(0, n_chunks, step, 0)

    for hh in range(hp):
        if has_state_out:
            for d in range(2):
                co_ref[d, hh] = cn_scr[hh * 2 + d, :, :dv]
                no_ref[d, hh] = jnp.transpose(cn_scr[hh * 2 + d, :, dv:])[0:1, :]
                mo_ref[d, hh] = m_scr[hh * 2 + d]
        cols = slice(hh * dv, (hh + 1) * dv)
        hs = hf_scr[:, cols] + hb_scr[:, cols]
        mu = jnp.mean(hs, axis=1, keepdims=True)
        hc = hs - mu
        var = jnp.mean(hc * hc, axis=1, keepdims=True)
        hn = hc * lax.rsqrt(var + LN_EPS) * ng_ref[:, cols]
        y_ref[:, cols] = (jax.nn.sigmoid(o_ref[:, cols].astype(F32)) * hn).astype(BF16)


def _mlstm_core(qkv, o, gates, b_gate, norm_g, *, batch, seq, row_blk0, init=None, state_out=None, y_prev=None):
    m = qkv.shape[0]
    h, dk, dv, L, hp = ML_HEADS, ML_DK, ML_DV, ML_CHUNK, ML_HEADS_PER_STEP
    hg = h // hp
    nc = seq // L
    r0 = row_blk0 * seq
    g4 = jnp.swapaxes(gates[r0:r0 + batch * seq].reshape(batch, seq, 4, h), 2, 3)
    grow = jnp.transpose(g4.reshape(batch, nc, L, hg, hp * 4), (0, 3, 1, 4, 2))
    brow = jnp.transpose(b_gate.reshape(4, h)).reshape(hg, hp * 4, 1)
    kq = (h * dk) // (hp * dk)
    kv = (2 * h * dk) // (hp * dv)
    in_specs = [
        pl.BlockSpec((seq, hp * dk), lambda b, g: (row_blk0 + b, g)),
        pl.BlockSpec((seq, hp * dk), lambda b, g: (row_blk0 + b, kq + g)),
        pl.BlockSpec((seq, hp * dv), lambda b, g: (row_blk0 + b, kv + g)),
        pl.BlockSpec((seq, hp * dv), lambda b, g: (row_blk0 + b, g)),
        pl.BlockSpec((None, None, nc, hp * 4, L), lambda b, g: (b, g, 0, 0, 0)),
        pl.BlockSpec((None, hp * 4, 1), lambda b, g: (g, 0, 0)),
        pl.BlockSpec((1, hp * dv), lambda b, g: (0, g)),
    ]
    args = [qkv, qkv, qkv, o, grow, brow, norm_g.reshape(1, h * dv)]
    if init is not None:
        c0, n0, m0, lyr = init
        nl = c0.shape[1]
        in_specs += [
            pl.BlockSpec((None, None, 2, hp, dk, dv), lambda b, g: (b, lyr, 0, g, 0, 0)),
            pl.BlockSpec((None, None, 2, hp, 1, dk), lambda b, g: (b, lyr, 0, g, 0, 0)),
            pl.BlockSpec((None, None, 2, hp, 1, 1), lambda b, g: (b, lyr, 0, g, 0, 0)),
        ]
        args += [c0, n0.reshape(batch, nl, 2, h, 1, dk), m0.reshape(batch, nl, 2, h, 1, 1)]
    out_specs = [pl.BlockSpec((seq, hp * dv), lambda b, g: (row_blk0 + b, g))]
    out_shape = [jax.ShapeDtypeStruct((m, h * dv), BF16)]
    prev = {} if y_prev is None else {0: y_prev}
    if state_out is not None:
        nl_out, slot, st_prev = state_out
        out_specs += [
            pl.BlockSpec((None, None, 2, hp, dk, dv), lambda b, g: (b, slot, 0, g, 0, 0)),
            pl.BlockSpec((None, None, 2, hp, 1, dk), lambda b, g: (b, slot, 0, g, 0, 0)),
            pl.BlockSpec((None, None, 2, hp, 1, 1), lambda b, g: (b, slot, 0, g, 0, 0)),
        ]
        out_shape += [
            jax.ShapeDtypeStruct((batch, nl_out, 2, h, dk, dv), F32),
            jax.ShapeDtypeStruct((batch, nl_out, 2, h, 1, dk), F32),
            jax.ShapeDtypeStruct((batch, nl_out, 2, h, 1, 1), F32),
        ]
        if st_prev is not None:
            prev.update({1: st_prev[0], 2: st_prev[1], 3: st_prev[2]})
    return _call_aliased(
        functools.partial(_mlstm_kernel, n_chunks=nc, hp=hp, has_init=init is not None,
                          has_state_out=state_out is not None),
        prev,
        grid=(batch, hg),
        in_specs=in_specs,
        args=args,
        out_specs=out_specs,
        out_shape=out_shape,
        scratch_shapes=[
            pltpu.VMEM((seq, hp * dv), F32),
            pltpu.VMEM((seq, hp * dv), F32),
            pltpu.VMEM((nc, hp * dk, L), F32),
            pltpu.VMEM((2 * hp, dk, dv + LANES), F32),
            pltpu.VMEM((2 * hp, 1, 1), F32),
        ],
        compiler_params=_params("parallel", "parallel"),
        name="mlstm_core",
    )


def _rope(x, cos_t, sin_t):
    lane = lax.broadcasted_iota(jnp.int32, x.shape, 1)
    first = (lane % (2 * ROPE_PAIRS)) < ROPE_PAIRS
    rot = jnp.where(first, pltpu.roll(x, x.shape[1] - ROPE_PAIRS, 1), pltpu.roll(x, ROPE_PAIRS, 1))
    return x * cos_t + rot * sin_t


def _rms(x, g):
    return x * lax.rsqrt(jnp.mean(x * x, axis=1, keepdims=True) + LN_EPS) * g


def _attn_kernel(*refs, seq, has_ctx):
    it = iter(refs)
    q_ref, k_ref, v_ref, qg_ref, kg_ref = (next(it) for _ in range(5))
    cos_ref = sin_ref = ck_ref = cv_ref = None
    if has_ctx:
        cos_ref, sin_ref, ck_ref, cv_ref = (next(it) for _ in range(4))
    o_ref = next(it)
    ko_ref = vo_ref = None
    if not has_ctx:
        ko_ref, vo_ref = next(it), next(it)
    keys_scr, vals_scr = next(it), next(it)

    hd = ATTN_HD
    kn = _rms(k_ref[...].astype(F32), kg_ref[...])
    if has_ctx:
        keys_scr[:seq, :] = _rope(kn, cos_ref[...], sin_ref[...]).astype(BF16)
        keys_scr[seq:, :] = ck_ref[...].astype(BF16)
        vals_scr[:seq, :] = v_ref[...]
        vals_scr[seq:, :] = cv_ref[...].astype(BF16)
    else:
        ko_ref[...] = kn
        vo_ref[...] = v_ref[...].astype(F32)
        keys_scr[...] = kn.astype(BF16)
        vals_scr[...] = v_ref[...]

    qb = min(ATTN_QBLOCK, seq)
    scale = hd ** -0.5

    def qblock(bi, carry):
        r0 = pl.multiple_of(bi * qb, qb)
        for g in range(ATTN_GROUP):
            q = _rms(q_ref[pl.ds(r0, qb), g * hd:(g + 1) * hd].astype(F32), qg_ref[...])
            if has_ctx:
                q = _rope(q, cos_ref[pl.ds(r0, qb), :], sin_ref[pl.ds(r0, qb), :])
            q = (q * scale).astype(BF16)
            s = lax.dot_general(q, keys_scr[...], (((1,), (1,)), ((), ())), preferred_element_type=F32)
            mx = jnp.max(s, axis=1, keepdims=True)
            p = jnp.exp(s - mx)
            den = jnp.sum(p, axis=1, keepdims=True)
            o = jnp.dot(p.astype(BF16), vals_scr[...], preferred_element_type=F32) / den
            o_ref[pl.ds(r0, qb), g * hd:(g + 1) * hd] = o.astype(BF16)
        return carry

    lax.fori_loop(0, seq // qb, qblock, 0)


def _attn_core(qkv, q_g, k_g, *, batch, seq, row_blk0, rope=None, ctx=None, y_prev=None):
    m = qkv.shape[0]
    hd, kvh, grp = ATTN_HD, ATTN_KV_HEADS, ATTN_GROUP
    has_ctx = ctx is not None
    k_off = ATTN_HEADS
    v_off = ATTN_HEADS + kvh
    in_specs = [
        pl.BlockSpec((seq, grp * hd), lambda b, kv: (row_blk0 + b, kv)),
        pl.BlockSpec((seq, hd), lambda b, kv: (row_blk0 + b, k_off + kv)),
        pl.BlockSpec((seq, hd), lambda b, kv: (row_blk0 + b, v_off + kv)),
        pl.BlockSpec((1, hd), lambda b, kv: (0, 0)),
        pl.BlockSpec((1, hd), lambda b, kv: (0, 0)),
    ]
    args = [qkv, qkv, qkv, q_g.reshape(1, hd), k_g.reshape(1, hd)]
    n_keys = seq
    if has_ctx:
        cos_t, sin_t = rope
        ck, cv = ctx
        past = ck.shape[1]
        n_keys = seq + past
        in_specs += [
            pl.BlockSpec((seq, hd), lambda b, kv: (0, 0)),
            pl.BlockSpec((seq, hd), lambda b, kv: (0, 0)),
            pl.BlockSpec((None, past, hd), lambda b, kv: (b, 0, kv)),
            pl.BlockSpec((None, past, hd), lambda b, kv: (b, 0, kv)),
        ]
        args += [cos_t, sin_t, ck, cv]
    out_specs = [pl.BlockSpec((seq, grp * hd), lambda b, kv: (row_blk0 + b, kv))]
    out_shape = [jax.ShapeDtypeStruct((m, ATTN_HEADS * hd), BF16)]
    if not has_ctx:
        out_specs += [pl.BlockSpec((None, seq, hd), lambda b, kv: (b, 0, kv))] * 2
        out_shape += [jax.ShapeDtypeStruct((batch, seq, kvh * hd), F32)] * 2
    return _call_aliased(
        functools.partial(_attn_kernel, seq=seq, has_ctx=has_ctx),
        {} if y_prev is None else {0: y_prev},
        grid=(batch, kvh),
        in_specs=in_specs,
        args=args,
        out_specs=out_specs,
        out_shape=out_shape,
        scratch_shapes=[pltpu.VMEM((n_keys, hd), BF16), pltpu.VMEM((n_keys, hd), BF16)],
        compiler_params=_params("parallel", "parallel"),
        name="attn_core",
    )


def _rope_tables(seq):
    rows = seq // GRID_W
    row = jnp.repeat(jnp.arange(rows, dtype=F32), GRID_W)
    col = jnp.tile(jnp.arange(GRID_W, dtype=F32), rows)
    freqs = ROPE_THETA ** (-jnp.arange(ROPE_PAIRS, dtype=F32) / ROPE_PAIRS)
    ar, ac = row[:, None] * freqs, col[:, None] * freqs
    cos_t = jnp.concatenate([jnp.cos(ar), jnp.cos(ar), jnp.cos(ac), jnp.cos(ac)], axis=1)
    sin_t = jnp.concatenate([-jnp.sin(ar), jnp.sin(ar), -jnp.sin(ac), jnp.sin(ac)], axis=1)
    return cos_t, sin_t


def _dwconv_silu(x, w, b):
    t = x.shape[0]
    row = lax.broadcasted_iota(jnp.int32, x.shape, 0)
    pad = SSD_CONV_W // 2
    acc = jnp.zeros_like(x) + b
    for j in range(SSD_CONV_W):
        off = j - pad
        if off == 0:
            xs = x
        else:
            xs = pltpu.roll(x, (-off) % t, 0)
            valid = (row + off >= 0) & (row + off < t)
            xs = jnp.where(valid, xs, 0.0)
        acc = acc + xs * w[j:j + 1, :]
    return _silu(acc)


def _ssd_kernel(*refs, n_chunks, has_init, has_state_out):
    it = iter(refs)
    (z_ref, x_ref, b_ref, c_ref, wx_ref, wb_ref, wc_ref, bx_ref, bb_ref, bc_ref,
     dtc_ref, dtr_ref, dbc_ref, dbr_ref, alc_ref, alr_ref, dsk_ref, ng_ref) = (next(it) for _ in range(18))
    h0_ref = next(it) if has_init else None
    y_ref, ssq_ref = next(it), next(it)
    ho_ref = next(it) if has_state_out else None
    xs_scr, xsb_scr, xst_scr, bm_scr, cm_scr, dtc_scr, y_scr, h_scr = (next(it) for _ in range(8))

    L, P, N = SSD_CHUNK, SSD_HD, SSD_N
    gw = x_ref.shape[1]
    E = gw // P
    n_pairs = gw // LANES

    xs = _dwconv_silu(x_ref[...].astype(F32), wx_ref[...], bx_ref[...])
    xs_scr[...] = xs
    xsb_scr[...] = xs.astype(BF16)
    for c in range(n_chunks):
        for j in range(n_pairs):
            xst_scr[c, j * LANES:(j + 1) * LANES, :] = jnp.transpose(
                xs_scr[c * L:(c + 1) * L, j * LANES:(j + 1) * LANES])
    bm_scr[...] = _dwconv_silu(b_ref[...].astype(F32), wb_ref[...], bb_ref[...]).astype(BF16)
    cm_scr[...] = _dwconv_silu(c_ref[...].astype(F32), wc_ref[...], bc_ref[...]).astype(BF16)
    dtc_scr[...] = _softplus(dtc_ref[...] + dbc_ref[...])
    a_c = -jnp.exp(alc_ref[...])
    a_r = -jnp.exp(alr_ref[...])

    t_idx = lax.broadcasted_iota(jnp.int32, (L, L), 0)
    s_idx = lax.broadcasted_iota(jnp.int32, (L, L), 1)
    lane_lo = lax.broadcasted_iota(jnp.int32, (L, LANES), 1) < P
    ex_row = lax.broadcasted_iota(jnp.int32, (2 * E, gw), 0)
    ex_head = jnp.right_shift(lax.broadcasted_iota(jnp.int32, (2 * E, gw), 1), P.bit_length() - 1)

    for d in range(2):
        incl = (s_idx <= t_idx) if d == 0 else (s_idx >= t_idx)
        incl_bf = jnp.where(incl, 1.0, 0.0).astype(BF16)
        incl_t_bf = jnp.where((t_idx <= s_idx) if d == 0 else (t_idx >= s_idx), 1.0, 0.0).astype(BF16)
        expand_bf = jnp.where(ex_row == ex_head + d * E, 1.0, 0.0).astype(BF16)
        last = L - 1 if d == 0 else 0
        if has_init:
            for e in range(E):
                h_scr[e * P:(e + 1) * P, :] = h0_ref[d, e]
        else:
            h_scr[...] = jnp.zeros_like(h_scr)

        def chunk(ci, carry, d=d, incl=incl, incl_bf=incl_bf, incl_t_bf=incl_t_bf, expand_bf=expand_bf, last=last):
            c = ci if d == 0 else n_chunks - 1 - ci
            r0 = pl.multiple_of(c * L, L)
            bc = bm_scr[pl.ds(r0, L), :]
            cc = cm_scr[pl.ds(r0, L), :]
            dt_r = _softplus(dtr_ref[c] + dbr_ref[...])
            da_c = dtc_scr[pl.ds(r0, L), :] * a_c
            da_r = dt_r * a_r
            cs_c = _dot_split(_split3(da_c), incl_bf, lhs_is_mask=True)
            cs_r = _dot_split(_split3(da_r), incl_t_bf, lhs_is_mask=False)
            cs_rd = cs_r[d * E:(d + 1) * E, :]
            dt_rd = dt_r[d * E:(d + 1) * E, :]
            tot_r = cs_rd[:, last:last + 1]
            ecs = _dot_split(_split2(jnp.exp(cs_c)), expand_bf, lhs_is_mask=False)
            cb = lax.dot_general(cc, bc, (((1,), (1,)), ((), ())), preferred_element_type=F32)
            ys = []
            for j in range(n_pairs):
                xp = xsb_scr[pl.ds(r0, L), j * LANES:(j + 1) * LANES]
                outs = []
                for q in range(LANES // P):
                    e = j * (LANES // P) + q
                    seg = cs_c[:, d * E + e:d * E + e + 1] - cs_rd[e:e + 1, :]
                    m_e = cb * jnp.exp(jnp.where(incl, seg, -jnp.inf)) * dt_rd[e:e + 1, :]
                    outs.append(jnp.dot(m_e.astype(BF16), xp, preferred_element_type=F32))
                ys.append(jnp.where(lane_lo, outs[0], outs[1]))
            h_all = h_scr[...]
            y = jnp.concatenate(ys, axis=1) + ecs * lax.dot_general(
                cc, h_all.astype(BF16), (((1,), (1,)), ((), ())), preferred_element_type=F32)
            if d == 0:
                y_scr[pl.ds(r0, L), :] = y
            else:
                y_scr[pl.ds(r0, L), :] += y
            wr = jnp.exp(tot_r - cs_rd) * dt_rd
            wr_big = jnp.concatenate([jnp.broadcast_to(wr[e:e + 1, :], (P, L)) for e in range(E)], axis=0)
            upd = jnp.dot((xst_scr[c] * wr_big).astype(BF16), bc, preferred_element_type=F32)
            etot = jnp.exp(tot_r)
            for e in range(E):
                h_scr[e * P:(e + 1) * P, :] = etot[e:e + 1, :] * h_all[e * P:(e + 1) * P, :] + upd[e * P:(e + 1) * P, :]
            return carry

        lax.fori_loop(0, n_chunks, chunk, 0)
        if has_state_out:
            for e in range(E):
                ho_ref[d, e] = h_scr[e * P:(e + 1) * P, :]

    y = y_scr[...] + dsk_ref[...] * xs_scr[...]
    yz = y * _silu(z_ref[...].astype(F32))
    ssq = jnp.sum(yz * yz, axis=1, keepdims=True)

    @pl.when(pl.program_id(1) == 0)
    def _():
        ssq_ref[...] = ssq

    @pl.when(pl.program_id(1) != 0)
    def _():
        ssq_ref[...] += ssq

    y_ref[...] = (yz * ng_ref[...]).astype(BF16)


def _ssd_core(zxbc, dt_raw, conv_w, conv_b, dt_bias, a_log, d_skip, norm_g, *,
              batch, seq, row_blk0, init=None, state_out=False, prev=None):
    m = zxbc.shape[0]
    G, P, N, L = SSD_GROUPS, SSD_HD, SSD_N, SSD_CHUNK
    di = d_skip.shape[0] * P
    heads = di // P
    E = heads // G
    gw = E * P
    nc = seq // L
    r0 = row_blk0 * seq
    dt4 = dt_raw[r0:r0 + batch * seq].reshape(batch, seq, 2, G, E)
    dtc = jnp.transpose(dt4, (0, 3, 1, 2, 4)).reshape(batch, G, seq, 2 * E)
    dtr = jnp.transpose(dt4.reshape(batch, nc, L, 2, G, E), (0, 4, 1, 3, 5, 2)).reshape(batch, G, nc, 2 * E, L)

    def per_group(v):
        v3 = jnp.transpose(v.reshape(2, G, E), (1, 0, 2)).reshape(G, 2 * E)
        return v3[:, None, :], v3[:, :, None]

    dbc, dbr = per_group(dt_bias)
    alc, alr = per_group(a_log)
    x_blk0 = di // gw
    b_blk0 = (2 * di) // N
    c_blk0 = (2 * di + G * N) // N
    in_specs = [
        pl.BlockSpec((seq, gw), lambda b, g: (row_blk0 + b, g)),
        pl.BlockSpec((seq, gw), lambda b, g: (row_blk0 + b, x_blk0 + g)),
        pl.BlockSpec((seq, N), lambda b, g: (row_blk0 + b, b_blk0 + g)),
        pl.BlockSpec((seq, N), lambda b, g: (row_blk0 + b, c_blk0 + g)),
        pl.BlockSpec((SSD_CONV_W, gw), lambda b, g: (0, g)),
        pl.BlockSpec((SSD_CONV_W, N), lambda b, g: (0, di // N + g)),
        pl.BlockSpec((SSD_CONV_W, N), lambda b, g: (0, di // N + G + g)),
        pl.BlockSpec((1, gw), lambda b, g: (0, g)),
        pl.BlockSpec((1, N), lambda b, g: (0, di // N + g)),
        pl.BlockSpec((1, N), lambda b, g: (0, di // N + G + g)),
        pl.BlockSpec((None, None, seq, 2 * E), lambda b, g: (b, g, 0, 0)),
        pl.BlockSpec((None, None, nc, 2 * E, L), lambda b, g: (b, g, 0, 0, 0)),
        pl.BlockSpec((None, 1, 2 * E), lambda b, g: (g, 0, 0)),
        pl.BlockSpec((None, 2 * E, 1), lambda b, g: (g, 0, 0)),
        pl.BlockSpec((None, 1, 2 * E), lambda b, g: (g, 0, 0)),
        pl.BlockSpec((None, 2 * E, 1), lambda b, g: (g, 0, 0)),
        pl.BlockSpec((1, gw), lambda b, g: (0, g)),
        pl.BlockSpec((1, gw), lambda b, g: (0, g)),
    ]
    cb2 = conv_b.reshape(1, -1)
    args = [zxbc, zxbc, zxbc, zxbc, conv_w, conv_w, conv_w, cb2, cb2, cb2,
            dtc, dtr, dbc, dbr, alc, alr,
            jnp.repeat(d_skip, P).reshape(1, di), norm_g.reshape(1, di)]
    if init is not None:
        in_specs.append(pl.BlockSpec((None, 2, E, P, N), lambda b, g: (b, 0, g, 0, 0)))
        args.append(init)
    out_specs = [
        pl.BlockSpec((seq, gw), lambda b, g: (row_blk0 + b, g)),
        pl.BlockSpec((seq, 1), lambda b, g: (row_blk0 + b, 0)),
    ]
    out_shape = [jax.ShapeDtypeStruct((m, di), BF16), jax.ShapeDtypeStruct((m, 1), F32)]
    if state_out:
        out_specs.append(pl.BlockSpec((None, 2, E, P, N), lambda b, g: (b, 0, g, 0, 0)))
        out_shape.append(jax.ShapeDtypeStruct((batch, 2, heads, P, N), F32))
    return _call_aliased(
        functools.partial(_ssd_kernel, n_chunks=nc, has_init=init is not None, has_state_out=state_out),
        {} if prev is None else {0: prev[0], 1: prev[1]},
        grid=(batch, G),
        in_specs=in_specs,
        args=args,
        out_specs=out_specs,
        out_shape=out_shape,
        scratch_shapes=[
            pltpu.VMEM((seq, gw), F32),
            pltpu.VMEM((seq, gw), BF16),
            pltpu.VMEM((nc, gw, L), F32),
            pltpu.VMEM((seq, N), BF16),
            pltpu.VMEM((seq, N), BF16),
            pltpu.VMEM((seq, 2 * E), F32),
            pltpu.VMEM((seq, gw), F32),
            pltpu.VMEM((E * P, N), F32),
        ],
        compiler_params=_params("parallel", "arbitrary"),
        name="ssd_core",
    )


def kernel(x_prompt, x_sample, cache_attn_k, cache_attn_v, state_mlstm_C, state_mlstm_n, state_mlstm_m, state_ssd_h, c, c_ctx, mod_w, mod_b, ln_g, ln_b, ffn_w_in, ffn_w_out, mlstm_w_in, mlstm_b_gate, mlstm_norm_g, mlstm_w_out, attn_w_qkv, attn_q_norm, attn_k_norm, attn_w_out, ssd_w_in, ssd_conv_w, ssd_conv_b, ssd_dt_bias, ssd_A_log, ssd_D, ssd_norm_g, ssd_w_out):
    bp, sp, d = x_prompt.shape
    bd, sd, _ = x_sample.shape
    depth = mod_w.shape[0]
    n_ctx = bp * sp
    m = n_ctx + bd * sd
    rows = _Rows(n_ctx, sd, d)
    alpha = (2 * depth) ** 0.25
    ctx_blk0, dec_blk0 = 0, n_ctx // sd

    cvec = jnp.zeros((N_SEG_PAD, d), F32).at[0].set(c_ctx).at[1:1 + bd].set(c)
    modv = _adaln(cvec, mod_w, mod_b).reshape(depth * N_SEG_PAD * N_MOD, 1, d)
    lnv = jnp.stack([ln_g, ln_b], axis=2).reshape(depth * 3 * 2, 1, d)

    x = jnp.concatenate([x_prompt.reshape(n_ctx, d), x_sample.reshape(bd * sd, d)], axis=0)
    h = _modulate(x, modv, rows, 0, 0, 1)

    qk_w, v_w = 2 * ML_HEADS * ML_DK, ML_HEADS * ML_DV
    mlstm_wt = jnp.swapaxes(mlstm_w_in, 1, 2)

    ml_states, attn_kv, ssd_states = None, [], []
    for i in range(depth):
        kind, j = i % 3, i // 3
        act = _ffn_in(h, ffn_w_in, (i, 0))
        x, h = _out_ln(act, ffn_w_out, (i, 0), x, modv, lnv, rows, layer=i, k_gate=2, ln_idx=3 * i,
                       alpha=alpha, coef=0.5, nxt=(i, 3, 4), name="ffn_out")
        ssq = None
        if kind == 0:
            qkv = _proj(h, mlstm_wt, (j,), 0, qk_w + v_w, BF16, scale_first=ML_DK ** -0.5,
                        bn=ML_HEADS * ML_DK, w_t=True, name="mlstm_qkv")
            og = _proj(h, mlstm_wt, (j,), qk_w + v_w, v_w, BF16, w_t=True, name="mlstm_o")
            n_gate = 4 * ML_HEADS
            gates = _small_proj(h, mlstm_wt, (j,), (qk_w + 2 * v_w) // n_gate, n_gate, w_t=True, name="mlstm_gates")
            y, *ml_states = _mlstm_core(qkv, og, gates, mlstm_b_gate[j], mlstm_norm_g[j],
                                        batch=bp, seq=sp, row_blk0=ctx_blk0,
                                        state_out=(mlstm_w_in.shape[0], j, ml_states))
            (y,) = _mlstm_core(qkv, og, gates, mlstm_b_gate[j], mlstm_norm_g[j],
                               batch=bd, seq=sd, row_blk0=dec_blk0, y_prev=y,
                               init=(state_mlstm_C, state_mlstm_n, state_mlstm_m, j))
            w_out = mlstm_w_out
        elif kind == 1:
            qkv = _proj(h, attn_w_qkv, (j,), 0, attn_w_qkv.shape[2], BF16, name="attn_qkv")
            y, kc, vc = _attn_core(qkv, attn_q_norm[j], attn_k_norm[j], batch=bp, seq=sp, row_blk0=ctx_blk0)
            past = cache_attn_k.shape[2]
            (y,) = _attn_core(qkv, attn_q_norm[j], attn_k_norm[j], batch=bd, seq=sd, row_blk0=dec_blk0,
                              rope=_rope_tables(sd), y_prev=y,
                              ctx=(cache_attn_k[:, j].reshape(bd, past, -1), cache_attn_v[:, j].reshape(bd, past, -1)))
            attn_kv.append((kc.reshape(bp, sp, ATTN_KV_HEADS, ATTN_HD), vc.reshape(bp, sp, ATTN_KV_HEADS, ATTN_HD)))
            w_out = attn_w_out
        else:
            di = ssd_D.shape[1] * SSD_HD
            zxbc_w = 2 * di + 2 * SSD_GROUPS * SSD_N
            zxbc = _proj(h, ssd_w_in, (j,), 0, zxbc_w, BF16, name="ssd_zxbc")
            dt_raw = _small_proj(h, ssd_w_in, (j,), zxbc_w // LANES, LANES, name="ssd_dt")
            y, ssq, hst = _ssd_core(zxbc, dt_raw, ssd_conv_w[j], ssd_conv_b[j], ssd_dt_bias[j], ssd_A_log[j],
                                    ssd_D[j], ssd_norm_g[j], batch=bp, seq=sp, row_blk0=ctx_blk0, state_out=True)
            y, ssq = _ssd_core(zxbc, dt_raw, ssd_conv_w[j], ssd_conv_b[j], ssd_dt_bias[j], ssd_A_log[j],
                               ssd_D[j], ssd_norm_g[j], batch=bd, seq=sd, row_blk0=dec_blk0,
                               init=state_ssd_h[:, j], prev=(y, ssq))
            ssd_states.append(hst)
            w_out = ssd_w_out
        x, h = _out_ln(y, w_out, (j,), x, modv, lnv, rows, layer=i, k_gate=5, ln_idx=3 * i + 1,
                       alpha=alpha, coef=1.0, nxt=(i, 6, 7), ssq=ssq,
                       ssq_dim=None if ssq is None else w_out.shape[1], name="mix_out")
        act = _ffn_in(h, ffn_w_in, (i, 1))
        nxt = (i + 1, 0, 1) if i + 1 < depth else None
        x, h = _out_ln(act, ffn_w_out, (i, 1), x, modv, lnv, rows, layer=i, k_gate=8, ln_idx=3 * i + 2,
                       alpha=alpha, coef=0.5, nxt=nxt, name="ffn_out")

    y_prompt = x[:n_ctx].reshape(bp, sp, d)
    y_sample = x[n_ctx:].reshape(bd, sd, d)
    new_k = jnp.stack([kv[0] for kv in attn_kv], axis=1)
    new_v = jnp.stack([kv[1] for kv in attn_kv], axis=1)
    new_c = ml_states[0]
    new_n = ml_states[1].reshape(ml_states[1].shape[:4] + (ML_DK,))
    new_m = ml_states[2].reshape(ml_states[2].shape[:4])
    new_h = jnp.stack(ssd_states, axis=1)
    return (y_prompt, y_sample, new_k, new_v, new_c, new_n, new_m, new_h)
```

```python
import functools
import math

import jax
import jax.numpy as jnp
from jax import lax
from jax.experimental import pallas as pl
from jax.experimental.pallas import tpu as pltpu

F32 = jnp.float32
BF16 = jnp.bfloat16

LN_EPS = 1e-6
N_MOD = 9
N_SEG_PAD = 8
LANES = 128
VMEM_LIMIT = 60 * 1024 * 1024
LN_ROWS = 256

ML_HEADS, ML_DK, ML_DV, ML_CHUNK = 8, 128, 256, 128
ML_HEADS_PER_STEP = 4
ATTN_HEADS, ATTN_KV_HEADS, ATTN_HD = 16, 4, 128
ATTN_GROUP = ATTN_HEADS // ATTN_KV_HEADS
ATTN_QBLOCK = 256
GRID_W = 64
ROPE_THETA = 10000.0
ROPE_PAIRS = ATTN_HD // 4
SSD_HD, SSD_GROUPS, SSD_N, SSD_CONV_W, SSD_CHUNK = 64, 8, 128, 5, 128
CONV_PAD_ROWS = 16


def _params(*sem):
    return pltpu.CompilerParams(dimension_semantics=sem, vmem_limit_bytes=VMEM_LIMIT)


def _lead_spec(lead, block, index_fn):
    lead = tuple(lead)
    return pl.BlockSpec((None,) * len(lead) + tuple(block), lambda *g: lead + tuple(index_fn(*g)))


def _call_aliased(kernel_fn, prev, *, in_specs, args, out_specs, out_shape, **kw):
    prev = dict(prev or {})
    n = len(prev)

    def body(*refs):
        return kernel_fn(*refs[n:])

    return pl.pallas_call(
        body,
        in_specs=[pl.BlockSpec(memory_space=pl.ANY)] * n + list(in_specs),
        out_specs=out_specs,
        out_shape=out_shape,
        input_output_aliases={i: o for i, o in enumerate(prev)},
        **kw,
    )(*prev.values(), *args)


def _silu(x):
    return x * jax.nn.sigmoid(x)


def _log_sigmoid(x):
    return jnp.minimum(x, 0.0) - jnp.log1p(jnp.exp(-jnp.abs(x)))


def _softplus(x):
    return jnp.maximum(x, 0.0) + jnp.log1p(jnp.exp(-jnp.abs(x)))


def _split_bf16(x, n):
    parts, r = [], x
    for _ in range(n):
        p = r.astype(BF16)
        parts.append(p)
        r = r - p.astype(F32)
    return parts


def _split3(x):
    return _split_bf16(x, 3)


def _split2(x):
    return _split_bf16(x, 2)


def _dot_split(parts, mask_bf, *, lhs_is_mask):
    acc = None
    for p in reversed(parts):
        t = (jnp.dot(mask_bf, p, preferred_element_type=F32) if lhs_is_mask
             else jnp.dot(p, mask_bf, preferred_element_type=F32))
        acc = t if acc is None else acc + t
    return acc


def _adaln_kernel(cv_ref, w_ref, b_ref, o_ref):
    s = _silu(cv_ref[...]).astype(BF16)
    w = w_ref[...].astype(BF16)
    o_ref[...] = jnp.dot(s, w, preferred_element_type=F32) + b_ref[...]


def _adaln(cvec, mod_w, mod_b, bn=1024):
    depth, d, n = mod_w.shape
    return pl.pallas_call(
        _adaln_kernel,
        grid=(depth, n // bn),
        in_specs=[
            pl.BlockSpec((N_SEG_PAD, d), lambda l, j: (0, 0)),
            pl.BlockSpec((None, d, bn), lambda l, j: (l, 0, j)),
            pl.BlockSpec((None, 1, bn), lambda l, j: (l, 0, j)),
        ],
        out_specs=pl.BlockSpec((None, N_SEG_PAD, bn), lambda l, j: (l, 0, j)),
        out_shape=jax.ShapeDtypeStruct((depth, N_SEG_PAD, n), F32),
        compiler_params=_params("arbitrary", "arbitrary"),
        name="adaln",
    )(cvec, mod_w, mod_b.reshape(depth, 1, n))


class _Rows:
    def __init__(self, n_ctx_rows, dec_seq, d_model):
        self.n_ctx = n_ctx_rows
        self.dec_seq = dec_seq
        self.d = d_model

    def seg(self, i, bm):
        r = i * bm
        return jnp.where(r < self.n_ctx, 0, 1 + (r - self.n_ctx) // self.dec_seq)


def _mod_spec(rows, layer, k, bm):
    def idx(i, *_):
        return ((layer * N_SEG_PAD + rows.seg(i, bm)) * N_MOD + k, 0, 0)

    return pl.BlockSpec((None, 1, rows.d), idx)


def _modulate_kernel(x_ref, sh_ref, sc_ref, h_ref):
    h_ref[...] = (x_ref[...] * (1.0 + sc_ref[...]) + sh_ref[...]).astype(BF16)


def _modulate(x, modv, rows, layer, k_shift, k_scale, bm=512):
    m, d = x.shape
    return pl.pallas_call(
        _modulate_kernel,
        grid=(m // bm,),
        in_specs=[
            pl.BlockSpec((bm, d), lambda i: (i, 0)),
            _mod_spec(rows, layer, k_shift, bm),
            _mod_spec(rows, layer, k_scale, bm),
        ],
        out_specs=pl.BlockSpec((bm, d), lambda i: (i, 0)),
        out_shape=jax.ShapeDtypeStruct((m, d), BF16),
        compiler_params=_params("parallel"),
        name="modulate",
    )(x, modv, modv)


def _ffn_in_kernel(a_ref, wg_ref, wu_ref, o_ref, w_scr):
    bn = wg_ref.shape[1]

    @pl.when(pl.program_id(1) == 0)
    def _():
        w_scr[:, :bn] = wg_ref[...].astype(BF16)
        w_scr[:, bn:] = wu_ref[...].astype(BF16)

    u = jnp.dot(a_ref[...], w_scr[...], preferred_element_type=F32)
    o_ref[...] = (_silu(u[:, :bn]) * u[:, bn:]).astype(BF16)


def _ffn_in(a, w_in, lead, bm=1024, bn=512):
    m, k = a.shape
    f = w_in.shape[-1] // 2
    nt = f // bn
    return pl.pallas_call(
        _ffn_in_kernel,
        grid=(nt, m // bm),
        in_specs=[
            pl.BlockSpec((bm, k), lambda j, i: (i, 0)),
            _lead_spec(lead, (k, bn), lambda j, i: (0, j)),
            _lead_spec(lead, (k, bn), lambda j, i: (0, j + nt)),
        ],
        out_specs=pl.BlockSpec((bm, bn), lambda j, i: (i, j)),
        out_shape=jax.ShapeDtypeStruct((m, f), BF16),
        scratch_shapes=[pltpu.VMEM((k, 2 * bn), BF16)],
        compiler_params=_params("arbitrary", "arbitrary"),
        name="ffn_in",
    )(a, w_in, w_in)


def _proj_kernel(a_ref, w_ref, o_ref, w_scr, *, scale_first, w_t):
    @pl.when(pl.program_id(1) == 0)
    def _():
        w = w_ref[...]
        w_scr[...] = (jnp.transpose(w) if w_t else w).astype(BF16)

    u = jnp.dot(a_ref[...], w_scr[...], preferred_element_type=F32)
    if scale_first is not None:
        u = u * jnp.where(pl.program_id(0) == 0, scale_first, 1.0)
    o_ref[...] = u.astype(o_ref.dtype)


def _proj(a, w, lead, col0, ncols, out_dtype, bm=1024, bn=1024, scale_first=None, w_t=False, name="proj"):
    m, k = a.shape
    assert col0 % bn == 0 and ncols % bn == 0
    j0 = col0 // bn
    w_spec = (_lead_spec(lead, (bn, k), lambda j, i: (j + j0, 0)) if w_t
              else _lead_spec(lead, (k, bn), lambda j, i: (0, j + j0)))
    return pl.pallas_call(
        functools.partial(_proj_kernel, scale_first=scale_first, w_t=w_t),
        grid=(ncols // bn, m // bm),
        in_specs=[
            pl.BlockSpec((bm, k), lambda j, i: (i, 0)),
            w_spec,
        ],
        out_specs=pl.BlockSpec((bm, bn), lambda j, i: (i, j)),
        out_shape=jax.ShapeDtypeStruct((m, ncols), out_dtype),
        scratch_shapes=[pltpu.VMEM((k, bn), BF16)],
        compiler_params=_params("arbitrary", "arbitrary"),
        name=name,
    )(a, w)


def _small_proj_kernel(a_ref, w_ref, o_ref, *, w_t):
    w = w_ref[...].astype(BF16)
    if w_t:
        o_ref[...] = lax.dot_general(a_ref[...], w, (((1,), (1,)), ((), ())), preferred_element_type=F32)
    else:
        o_ref[...] = jnp.dot(a_ref[...], w, preferred_element_type=F32)


def _small_proj(a, w, lead, col_block, ncols, bm=1024, w_t=False, name="small_proj"):
    m, k = a.shape
    w_spec = (_lead_spec(lead, (ncols, k), lambda i: (col_block, 0)) if w_t
              else _lead_spec(lead, (k, ncols), lambda i: (0, col_block)))
    return pl.pallas_call(
        functools.partial(_small_proj_kernel, w_t=w_t),
        grid=(m // bm,),
        in_specs=[
            pl.BlockSpec((bm, k), lambda i: (i, 0)),
            w_spec,
        ],
        out_specs=pl.BlockSpec((bm, ncols), lambda i: (i, 0)),
        out_shape=jax.ShapeDtypeStruct((m, ncols), F32),
        compiler_params=_params("parallel"),
        name=name,
    )(a, w)


def _out_ln_kernel(*refs, alpha, coef, has_next, has_ssq, ssq_dim):
    it = iter(refs)
    a_ref, w_ref, x_ref, gate_ref, g_ref, b_ref = (next(it) for _ in range(6))
    sh_ref = sc_ref = ssq_ref = None
    if has_next:
        sh_ref, sc_ref = next(it), next(it)
    if has_ssq:
        ssq_ref = next(it)
    xo_ref = next(it)
    ho_ref = next(it) if has_next else None

    k = pl.program_id(1)

    @pl.when(k == 0)
    def _():
        xo_ref[...] = jnp.dot(a_ref[...], w_ref[...].astype(BF16), preferred_element_type=F32)

    @pl.when(k != 0)
    def _():
        xo_ref[...] += jnp.dot(a_ref[...], w_ref[...].astype(BF16), preferred_element_type=F32)

    @pl.when(k == pl.num_programs(1) - 1)
    def _():
        cg = coef * gate_ref[...]

        def slab(si, carry):
            rs = pl.ds(pl.multiple_of(si * LN_ROWS, LN_ROWS), LN_ROWS)
            y = xo_ref[rs, :]
            if has_ssq:
                y = y * lax.rsqrt(ssq_ref[rs, :] * (1.0 / ssq_dim) + LN_EPS)
            z = alpha * x_ref[rs, :] + cg * y
            mu = jnp.mean(z, axis=1, keepdims=True)
            zc = z - mu
            var = jnp.mean(zc * zc, axis=1, keepdims=True)
            xn = zc * lax.rsqrt(var + LN_EPS) * g_ref[...] + b_ref[...]
            xo_ref[rs, :] = xn
            if has_next:
                ho_ref[rs, :] = (xn * (1.0 + sc_ref[...]) + sh_ref[...]).astype(BF16)
            return carry

        lax.fori_loop(0, xo_ref.shape[0] // LN_ROWS, slab, 0)


def _out_ln(a, w, lead, x, modv, lnv, rows, *, layer, k_gate, ln_idx, alpha, coef,
            nxt=None, ssq=None, ssq_dim=None, bm=1024, bk=512, name="out_ln"):
    m, kdim = a.shape
    d = x.shape[1]
    assert kdim % bk == 0
    in_specs = [
        pl.BlockSpec((bm, bk), lambda i, k: (i, k)),
        _lead_spec(lead, (bk, d), lambda i, k: (k, 0)),
        pl.BlockSpec((bm, d), lambda i, k: (i, 0)),
        _mod_spec(rows, layer, k_gate, bm),
        pl.BlockSpec((None, 1, d), lambda i, k: (2 * ln_idx, 0, 0)),
        pl.BlockSpec((None, 1, d), lambda i, k: (2 * ln_idx + 1, 0, 0)),
    ]
    args = [a, w, x, modv, lnv, lnv]
    if nxt is not None:
        in_specs += [_mod_spec(rows, nxt[0], nxt[1], bm), _mod_spec(rows, nxt[0], nxt[2], bm)]
        args += [modv, modv]
    if ssq is not None:
        in_specs.append(pl.BlockSpec((bm, ssq.shape[1]), lambda i, k: (i, 0)))
        args.append(ssq)
    out_specs = [pl.BlockSpec((bm, d), lambda i, k: (i, 0))]
    out_shape = [jax.ShapeDtypeStruct((m, d), F32)]
    if nxt is not None:
        out_specs.append(pl.BlockSpec((bm, d), lambda i, k: (i, 0)))
        out_shape.append(jax.ShapeDtypeStruct((m, d), BF16))
    res = pl.pallas_call(
        functools.partial(_out_ln_kernel, alpha=alpha, coef=coef, has_next=nxt is not None,
                          has_ssq=ssq is not None, ssq_dim=ssq_dim),
        grid=(m // bm, kdim // bk),
        in_specs=in_specs,
        out_specs=out_specs,
        out_shape=out_shape,
        compiler_params=_params("parallel", "arbitrary"),
        name=name,
    )(*args)
    return (res[0], res[1]) if nxt is not None else (res[0], None)


def _mlstm_kernel(*refs, n_chunks, hp, has_init, has_state_out):
    it = iter(refs)
    q_ref, k_ref, v_ref, o_ref, gr_ref, br_ref, ng_ref = (next(it) for _ in range(7))
    c0_ref = n0_ref = m0_ref = None
    if has_init:
        c0_ref, n0_ref, m0_ref = next(it), next(it), next(it)
    y_ref = next(it)
    co_ref = no_ref = mo_ref = None
    if has_state_out:
        co_ref, no_ref, mo_ref = next(it), next(it), next(it)
    hf_scr, hb_scr, kt_scr, vx_scr, cn_scr, m_scr = (next(it) for _ in range(6))

    L, dk, dv = ML_CHUNK, ML_DK, ML_DV
    t_idx = lax.broadcasted_iota(jnp.int32, (L, L), 0)
    s_idx = lax.broadcasted_iota(jnp.int32, (L, L), 1)
    ones_bf = jnp.ones((L, LANES), BF16)
    masks = []
    for d in range(2):
        incl = (s_idx <= t_idx) if d == 0 else (s_idx >= t_idx)
        incl_t = (t_idx <= s_idx) if d == 0 else (t_idx >= s_idx)
        masks.append((incl, jnp.where(incl_t, 1.0, 0.0).astype(BF16)))

    for hh in range(hp):
        vx_scr[:, hh * (dv + LANES):hh * (dv + LANES) + dv] = v_ref[:, hh * dv:(hh + 1) * dv]
        vx_scr[:, hh * (dv + LANES) + dv:(hh + 1) * (dv + LANES)] = jnp.ones((v_ref.shape[0], LANES), BF16)
        for c in range(n_chunks):
            kt_scr[c, hh * dk:(hh + 1) * dk, :] = jnp.transpose(
                k_ref[c * L:(c + 1) * L, hh * dk:(hh + 1) * dk].astype(F32))
        for d in range(2):
            i = hh * 2 + d
            if has_init:
                cn_scr[i, :, :dv] = c0_ref[d, hh]
                cn_scr[i, :, dv:] = jnp.transpose(jnp.broadcast_to(n0_ref[d, hh], (LANES, dk)))
                m_scr[i] = m0_ref[d, hh]
            else:
                cn_scr[i] = jnp.zeros((dk, dv + LANES), F32)
                m_scr[i] = jnp.zeros((1, 1), F32)

    def step(ci, carry):
        chains = []
        for d in range(2):
            incl, incl_t_bf = masks[d]
            c = ci if d == 0 else n_chunks - 1 - ci
            r0 = pl.multiple_of(c * L, L)
            last = L - 1 if d == 0 else 0
            grow = gr_ref[c] + br_ref[...]
            lf_parts = _split3(_log_sigmoid(grow))
            cum_r = _dot_split(lf_parts, incl_t_bf, lhs_is_mask=False)
            lf_parts = [p.astype(F32) for p in lf_parts]
            for hh in range(hp):
                gi, gf = hh * 4 + d, hh * 4 + 2 + d
                ch = dict(d=d, hh=hh, i=hh * 2 + d, c=c, r0=r0, incl=incl,
                          li_r=grow[gi:gi + 1, :], b_r=cum_r[gf:gf + 1, :],
                          lf_rows=[p[gf:gf + 1, :] for p in lf_parts])
                ch["g"] = ch["b_r"][:, last:last + 1]
                chains.append(ch)

        for ch in chains:
            b_c = None
            for p in reversed(ch["lf_rows"]):
                t = jnp.dot(jnp.where(ch["incl"], p, 0.0).astype(BF16), ones_bf, preferred_element_type=F32)
                b_c = t if b_c is None else b_c + t
            ch["b_c"] = b_c
        for ch in chains:
            hh, r0 = ch["hh"], ch["r0"]
            qc = q_ref[pl.ds(r0, L), hh * dk:(hh + 1) * dk]
            kc = k_ref[pl.ds(r0, L), hh * dk:(hh + 1) * dk]
            ch["cn_prev"] = cn_scr[ch["i"]]
            ch["m_prev"] = m_scr[ch["i"]]
            ch["qk"] = lax.dot_general(qc, kc, (((1,), (1,)), ((), ())), preferred_element_type=F32)
            ch["q_cn"] = jnp.dot(qc, ch["cn_prev"].astype(BF16), preferred_element_type=F32)
        for ch in chains:
            hh, r0 = ch["hh"], ch["r0"]
            dmat = jnp.where(ch["incl"], ch["b_c"] - ch["b_r"] + ch["li_r"], -jnp.inf)
            inter = ch["b_c"] + ch["m_prev"]
            m_t = jnp.maximum(inter, jnp.max(dmat, axis=1, keepdims=True))
            s_bf = (ch["qk"] * jnp.exp(dmat - m_t)).astype(BF16)
            ch["sc"] = jnp.exp(inter - m_t)
            ch["floor"] = jnp.exp(-m_t)
            ch["vc"] = vx_scr[pl.ds(r0, L), hh * (dv + LANES):(hh + 1) * (dv + LANES)]
            ch["s_v"] = jnp.dot(s_bf, ch["vc"], preferred_element_type=F32)
        for ch in chains:
            hh, r0, sc, q_cn, s_v = ch["hh"], ch["r0"], ch["sc"], ch["q_cn"], ch["s_v"]
            hs_scr = hf_scr if ch["d"] == 0 else hb_scr
            den = sc * q_cn[:, dv:] + s_v[:, dv:]
            inv = 1.0 / jnp.maximum(jnp.abs(den), ch["floor"])
            for cb in range(dv // LANES):
                cs = slice(cb * LANES, (cb + 1) * LANES)
                hs_scr[pl.ds(r0, L), hh * dv + cb * LANES:hh * dv + (cb + 1) * LANES] = (
                    sc * q_cn[:, cs] + s_v[:, cs]) * inv
        for ch in chains:
            hh, g, m_prev = ch["hh"], ch["g"], ch["m_prev"]
            ds_r = g - ch["b_r"] + ch["li_r"]
            m_new = jnp.maximum(g + m_prev, jnp.max(ds_r, axis=1, keepdims=True))
            ktw = (kt_scr[ch["c"], hh * dk:(hh + 1) * dk, :] * jnp.exp(ds_r - m_new)).astype(BF16)
            decay = jnp.exp(g + m_prev - m_new)
            cn_scr[ch["i"]] = decay * ch["cn_prev"] + jnp.dot(ktw, ch["vc"], preferred_element_type=F32)
            m_scr[ch["i"]] = m_new
        return carry

    lax.fori_loop(0, n_chunks, step, 0)

    for hh in range(hp):
        if has_state_out:
            for d in range(2):
                co_ref[d, hh] = cn_scr[hh * 2 + d, :, :dv]
                no_ref[d, hh] = jnp.transpose(cn_scr[hh * 2 + d, :, dv:])[0:1, :]
                mo_ref[d, hh] = m_scr[hh * 2 + d]
        cols = slice(hh * dv, (hh + 1) * dv)
        hs = hf_scr[:, cols] + hb_scr[:, cols]
        mu = jnp.mean(hs, axis=1, keepdims=True)
        hc = hs - mu
        var = jnp.mean(hc * hc, axis=1, keepdims=True)
        hn = hc * lax.rsqrt(var + LN_EPS) * ng_ref[:, cols]
        y_ref[:, cols] = (jax.nn.sigmoid(o_ref[:, cols].astype(F32)) * hn).astype(BF16)


def _mlstm_core(qkv, o, gates, b_gate, norm_g, *, batch, seq, row_blk0, init=None, state_out=None, y_prev=None):
    m = qkv.shape[0]
    h, dk, dv, L, hp = ML_HEADS, ML_DK, ML_DV, ML_CHUNK, ML_HEADS_PER_STEP
    hg = h // hp
    nc = seq // L
    r0 = row_blk0 * seq
    g4 = jnp.swapaxes(gates[r0:r0 + batch * seq].reshape(batch, seq, 4, h), 2, 3)
    grow = jnp.transpose(g4.reshape(batch, nc, L, hg, hp * 4), (0, 3, 1, 4, 2))
    brow = jnp.transpose(b_gate.reshape(4, h)).reshape(hg, hp * 4, 1)
    kq = (h * dk) // (hp * dk)
    kv = (2 * h * dk) // (hp * dv)
    in_specs = [
        pl.BlockSpec((seq, hp * dk), lambda b, g: (row_blk0 + b, g)),
        pl.BlockSpec((seq, hp * dk), lambda b, g: (row_blk0 + b, kq + g)),
        pl.BlockSpec((seq, hp * dv), lambda b, g: (row_blk0 + b, kv + g)),
        pl.BlockSpec((seq, hp * dv), lambda b, g: (row_blk0 + b, g)),
        pl.BlockSpec((None, None, nc, hp * 4, L), lambda b, g: (b, g, 0, 0, 0)),
        pl.BlockSpec((None, hp * 4, 1), lambda b, g: (g, 0, 0)),
        pl.BlockSpec((1, hp * dv), lambda b, g: (0, g)),
    ]
    args = [qkv, qkv, qkv, o, grow, brow, norm_g.reshape(1, h * dv)]
    if init is not None:
        c0, n0, m0, lyr = init
        nl = c0.shape[1]
        in_specs += [
            pl.BlockSpec((None, None, 2, hp, dk, dv), lambda b, g: (b, lyr, 0, g, 0, 0)),
            pl.BlockSpec((None, None, 2, hp, 1, dk), lambda b, g: (b, lyr, 0, g, 0, 0)),
            pl.BlockSpec((None, None, 2, hp, 1, 1), lambda b, g: (b, lyr, 0, g, 0, 0)),
        ]
        args += [c0, n0.reshape(batch, nl, 2, h, 1, dk), m0.reshape(batch, nl, 2, h, 1, 1)]
    out_specs = [pl.BlockSpec((seq, hp * dv), lambda b, g: (row_blk0 + b, g))]
    out_shape = [jax.ShapeDtypeStruct((m, h * dv), BF16)]
    prev = {} if y_prev is None else {0: y_prev}
    if state_out is not None:
        nl_out, slot, st_prev = state_out
        out_specs += [
            pl.BlockSpec((None, None, 2, hp, dk, dv), lambda b, g: (b, slot, 0, g, 0, 0)),
            pl.BlockSpec((None, None, 2, hp, 1, dk), lambda b, g: (b, slot, 0, g, 0, 0)),
            pl.BlockSpec((None, None, 2, hp, 1, 1), lambda b, g: (b, slot, 0, g, 0, 0)),
        ]
        out_shape += [
            jax.ShapeDtypeStruct((batch, nl_out, 2, h, dk, dv), F32),
            jax.ShapeDtypeStruct((batch, nl_out, 2, h, 1, dk), F32),
            jax.ShapeDtypeStruct((batch, nl_out, 2, h, 1, 1), F32),
        ]
        if st_prev is not None:
            prev.update({1: st_prev[0], 2: st_prev[1], 3: st_prev[2]})
    return _call_aliased(
        functools.partial(_mlstm_kernel, n_chunks=nc, hp=hp, has_init=init is not None,
                          has_state_out=state_out is not None),
        prev,
        grid=(batch, hg),
        in_specs=in_specs,
        args=args,
        out_specs=out_specs,
        out_shape=out_shape,
        scratch_shapes=[
            pltpu.VMEM((seq, hp * dv), F32),
            pltpu.VMEM((seq, hp * dv), F32),
            pltpu.VMEM((nc, hp * dk, L), F32),
            pltpu.VMEM((seq, hp * (dv + LANES)), BF16),
            pltpu.VMEM((2 * hp, dk, dv + LANES), F32),
            pltpu.VMEM((2 * hp, 1, 1), F32),
        ],
        compiler_params=_params("parallel", "parallel"),
        name="mlstm_core",
    )


def _rope(x, cos_t, sin_t):
    lane = lax.broadcasted_iota(jnp.int32, x.shape, 1)
    first = (lane % (2 * ROPE_PAIRS)) < ROPE_PAIRS
    rot = jnp.where(first, pltpu.roll(x, x.shape[1] - ROPE_PAIRS, 1), pltpu.roll(x, ROPE_PAIRS, 1))
    return x * cos_t + rot * sin_t


def _rms(x, g):
    return x * lax.rsqrt(jnp.mean(x * x, axis=1, keepdims=True) + LN_EPS) * g


def _attn_kernel(*refs, seq, has_ctx):
    it = iter(refs)
    q_ref, k_ref, v_ref, qg_ref, kg_ref = (next(it) for _ in range(5))
    cos_ref = sin_ref = ck_ref = cv_ref = None
    if has_ctx:
        cos_ref, sin_ref, ck_ref, cv_ref = (next(it) for _ in range(4))
    o_ref = next(it)
    ko_ref = vo_ref = None
    if not has_ctx:
        ko_ref, vo_ref = next(it), next(it)
    keys_scr, vals_scr = next(it), next(it)

    hd = ATTN_HD
    kn = _rms(k_ref[...].astype(F32), kg_ref[...])
    if has_ctx:
        keys_scr[:seq, :] = _rope(kn, cos_ref[...], sin_ref[...]).astype(BF16)
        keys_scr[seq:, :] = ck_ref[...].astype(BF16)
        vals_scr[:seq, :] = v_ref[...]
        vals_scr[seq:, :] = cv_ref[...].astype(BF16)
    else:
        ko_ref[...] = kn
        vo_ref[...] = v_ref[...].astype(F32)
        keys_scr[...] = kn.astype(BF16)
        vals_scr[...] = v_ref[...]

    qb = min(ATTN_QBLOCK, seq)
    scale = hd ** -0.5

    def qblock(bi, carry):
        r0 = pl.multiple_of(bi * qb, qb)
        ss = []
        for g in range(ATTN_GROUP):
            q = _rms(q_ref[pl.ds(r0, qb), g * hd:(g + 1) * hd].astype(F32), qg_ref[...])
            if has_ctx:
                q = _rope(q, cos_ref[pl.ds(r0, qb), :], sin_ref[pl.ds(r0, qb), :])
            q = (q * scale).astype(BF16)
            ss.append(lax.dot_general(q, keys_scr[...], (((1,), (1,)), ((), ())), preferred_element_type=F32))
        ps, dens = [], []
        for s in ss:
            p = jnp.exp(s - jnp.max(s, axis=1, keepdims=True))
            dens.append(jnp.sum(p, axis=1, keepdims=True))
            ps.append(p.astype(BF16))
        for g in range(ATTN_GROUP):
            o = jnp.dot(ps[g], vals_scr[...], preferred_element_type=F32) / dens[g]
            o_ref[pl.ds(r0, qb), g * hd:(g + 1) * hd] = o.astype(BF16)
        return carry

    lax.fori_loop(0, seq // qb, qblock, 0)


def _attn_core(qkv, q_g, k_g, *, batch, seq, row_blk0, rope=None, ctx=None, y_prev=None):
    m = qkv.shape[0]
    hd, kvh, grp = ATTN_HD, ATTN_KV_HEADS, ATTN_GROUP
    has_ctx = ctx is not None
    k_off = ATTN_HEADS
    v_off = ATTN_HEADS + kvh
    in_specs = [
        pl.BlockSpec((seq, grp * hd), lambda b, kv: (row_blk0 + b, kv)),
        pl.BlockSpec((seq, hd), lambda b, kv: (row_blk0 + b, k_off + kv)),
        pl.BlockSpec((seq, hd), lambda b, kv: (row_blk0 + b, v_off + kv)),
        pl.BlockSpec((1, hd), lambda b, kv: (0, 0)),
        pl.BlockSpec((1, hd), lambda b, kv: (0, 0)),
    ]
    args = [qkv, qkv, qkv, q_g.reshape(1, hd), k_g.reshape(1, hd)]
    n_keys = seq
    if has_ctx:
        cos_t, sin_t = rope
        ck, cv = ctx
        past = ck.shape[1]
        n_keys = seq + past
        in_specs += [
            pl.BlockSpec((seq, hd), lambda b, kv: (0, 0)),
            pl.BlockSpec((seq, hd), lambda b, kv: (0, 0)),
            pl.BlockSpec((None, past, hd), lambda b, kv: (b, 0, kv)),
            pl.BlockSpec((None, past, hd), lambda b, kv: (b, 0, kv)),
        ]
        args += [cos_t, sin_t, ck, cv]
    out_specs = [pl.BlockSpec((seq, grp * hd), lambda b, kv: (row_blk0 + b, kv))]
    out_shape = [jax.ShapeDtypeStruct((m, ATTN_HEADS * hd), BF16)]
    if not has_ctx:
        out_specs += [pl.BlockSpec((None, seq, hd), lambda b, kv: (b, 0, kv))] * 2
        out_shape += [jax.ShapeDtypeStruct((batch, seq, kvh * hd), F32)] * 2
    return _call_aliased(
        functools.partial(_attn_kernel, seq=seq, has_ctx=has_ctx),
        {} if y_prev is None else {0: y_prev},
        grid=(batch, kvh),
        in_specs=in_specs,
        args=args,
        out_specs=out_specs,
        out_shape=out_shape,
        scratch_shapes=[pltpu.VMEM((n_keys, hd), BF16), pltpu.VMEM((n_keys, hd), BF16)],
        compiler_params=_params("parallel", "parallel"),
        name="attn_core",
    )


def _rope_tables(seq):
    rows = seq // GRID_W
    row = jnp.repeat(jnp.arange(rows, dtype=F32), GRID_W)
    col = jnp.tile(jnp.arange(GRID_W, dtype=F32), rows)
    freqs = ROPE_THETA ** (-jnp.arange(ROPE_PAIRS, dtype=F32) / ROPE_PAIRS)
    ar, ac = row[:, None] * freqs, col[:, None] * freqs
    cos_t = jnp.concatenate([jnp.cos(ar), jnp.cos(ar), jnp.cos(ac), jnp.cos(ac)], axis=1)
    sin_t = jnp.concatenate([-jnp.sin(ar), jnp.sin(ar), -jnp.sin(ac), jnp.sin(ac)], axis=1)
    return cos_t, sin_t


def _dwconv_silu_chunk(xpad_ref, r0, w, b):
    L, padr = SSD_CHUNK, CONV_PAD_ROWS
    win = xpad_ref[pl.ds(r0, L + 2 * padr), :]
    t_idx = lax.broadcasted_iota(jnp.int32, (L, L + 2 * padr), 0)
    s_idx = lax.broadcasted_iota(jnp.int32, (L, L + 2 * padr), 1)
    acc = b
    for j in range(SSD_CONV_W):
        off = j - SSD_CONV_W // 2
        if off == 0:
            xs = win[padr:padr + L, :].astype(F32)
        else:
            shift = jnp.where(s_idx == t_idx + (padr + off), 1.0, 0.0).astype(BF16)
            xs = jnp.dot(shift, win, preferred_element_type=F32)
        acc = acc + xs * w[j:j + 1, :]
    return _silu(acc)


def _ssd_kernel(*refs, n_chunks, has_init, has_state_out):
    it = iter(refs)
    (z_ref, x_ref, b_ref, c_ref, wx_ref, wb_ref, wc_ref, bx_ref, bb_ref, bc_ref,
     dtc_ref, dtr_ref, dbc_ref, dbr_ref, alc_ref, alr_ref, dsk_ref, ng_ref) = (next(it) for _ in range(18))
    h0_ref = next(it) if has_init else None
    y_ref, ssq_ref = next(it), next(it)
    ho_ref = next(it) if has_state_out else None
    (xpad_scr, xs_scr, xsb_scr, xst_scr, bm_scr, cm_scr, dtc_scr,
     yf_scr, yb_scr, h_scr) = (next(it) for _ in range(10))

    L, P, N = SSD_CHUNK, SSD_HD, SSD_N
    gw = x_ref.shape[1]
    E = gw // P
    n_pairs = gw // LANES
    seq, padr = x_ref.shape[0], CONV_PAD_ROWS

    cw = gw + 2 * N
    xpad_scr[0:padr, :] = jnp.zeros((padr, cw), BF16)
    xpad_scr[padr + seq:, :] = jnp.zeros((padr, cw), BF16)
    xpad_scr[padr:padr + seq, 0:gw] = x_ref[...]
    xpad_scr[padr:padr + seq, gw:gw + N] = b_ref[...]
    xpad_scr[padr:padr + seq, gw + N:] = c_ref[...]
    conv_w = jnp.concatenate([wx_ref[...], wb_ref[...], wc_ref[...]], axis=1)
    conv_b = jnp.concatenate([bx_ref[...], bb_ref[...], bc_ref[...]], axis=1)
    for c in range(n_chunks):
        rows = slice(c * L, (c + 1) * L)
        xbc = _dwconv_silu_chunk(xpad_scr, c * L, conv_w, conv_b)
        xs_scr[rows, :] = xbc[:, :gw]
        xsb_scr[rows, :] = xbc[:, :gw].astype(BF16)
        bm_scr[rows, :] = xbc[:, gw:gw + N].astype(BF16)
        cm_scr[rows, :] = xbc[:, gw + N:].astype(BF16)
        for j in range(n_pairs):
            xst_scr[c, j * LANES:(j + 1) * LANES, :] = jnp.transpose(xbc[:, j * LANES:(j + 1) * LANES])
    dtc_scr[...] = _softplus(dtc_ref[...] + dbc_ref[...])
    a_c = -jnp.exp(alc_ref[...])
    a_r = -jnp.exp(alr_ref[...])

    t_idx = lax.broadcasted_iota(jnp.int32, (L, L), 0)
    s_idx = lax.broadcasted_iota(jnp.int32, (L, L), 1)
    lane_lo = lax.broadcasted_iota(jnp.int32, (L, LANES), 1) < P
    ex_row = lax.broadcasted_iota(jnp.int32, (2 * E, gw), 0)
    ex_head = jnp.right_shift(lax.broadcasted_iota(jnp.int32, (2 * E, gw), 1), P.bit_length() - 1)

    consts = []
    for d in range(2):
        incl = (s_idx <= t_idx) if d == 0 else (s_idx >= t_idx)
        incl_bf = jnp.where(incl, 1.0, 0.0).astype(BF16)
        incl_t_bf = jnp.where((t_idx <= s_idx) if d == 0 else (t_idx >= s_idx), 1.0, 0.0).astype(BF16)
        expand_bf = jnp.where(ex_row == ex_head + d * E, 1.0, 0.0).astype(BF16)
        consts.append((incl, incl_bf, incl_t_bf, expand_bf))
        if has_init:
            for e in range(E):
                h_scr[d, e * P:(e + 1) * P, :] = h0_ref[d, e]
        else:
            h_scr[d] = jnp.zeros((E * P, N), F32)

    def chunk(ci, carry):
        sd = []
        for d in range(2):
            incl, incl_bf, incl_t_bf, expand_bf = consts[d]
            last = L - 1 if d == 0 else 0
            c = ci if d == 0 else n_chunks - 1 - ci
            r0 = pl.multiple_of(c * L, L)
            s = dict(d=d, c=c, r0=r0, incl=incl, bc=bm_scr[pl.ds(r0, L), :], cc=cm_scr[pl.ds(r0, L), :],
                     h_all=h_scr[d])
            dt_r = _softplus(dtr_ref[c] + dbr_ref[...])
            da_c = dtc_scr[pl.ds(r0, L), :] * a_c
            da_r = dt_r * a_r
            s["cs_c"] = _dot_split(_split3(da_c), incl_bf, lhs_is_mask=True)
            cs_r = _dot_split(_split3(da_r), incl_t_bf, lhs_is_mask=False)
            s["cs_rd"] = cs_r[d * E:(d + 1) * E, :]
            s["dt_rd"] = dt_r[d * E:(d + 1) * E, :]
            s["tot_r"] = s["cs_rd"][:, last:last + 1]
            s["cb"] = lax.dot_general(s["cc"], s["bc"], (((1,), (1,)), ((), ())), preferred_element_type=F32)
            s["y_off"] = lax.dot_general(s["cc"], s["h_all"].astype(BF16), (((1,), (1,)), ((), ())),
                                         preferred_element_type=F32)
            s["expand_bf"] = expand_bf
            sd.append(s)
        for s in sd:
            s["ecs"] = _dot_split(_split2(jnp.exp(s["cs_c"])), s["expand_bf"], lhs_is_mask=False)
        for s in sd:
            d = s["d"]
            s["m"] = []
            for e in range(E):
                seg = s["cs_c"][:, d * E + e:d * E + e + 1] - s["cs_rd"][e:e + 1, :]
                m_e = s["cb"] * jnp.exp(jnp.where(s["incl"], seg, -jnp.inf)) * s["dt_rd"][e:e + 1, :]
                s["m"].append(m_e.astype(BF16))
        for s in sd:
            ys = []
            for j in range(n_pairs):
                xp = xsb_scr[pl.ds(s["r0"], L), j * LANES:(j + 1) * LANES]
                outs = [jnp.dot(s["m"][j * (LANES // P) + q], xp, preferred_element_type=F32)
                        for q in range(LANES // P)]
                ys.append(jnp.where(lane_lo, outs[0], outs[1]))
            y_scr = yf_scr if s["d"] == 0 else yb_scr
            y_scr[pl.ds(s["r0"], L), :] = jnp.concatenate(ys, axis=1) + s["ecs"] * s["y_off"]
        for s in sd:
            d, tot_r, h_all = s["d"], s["tot_r"], s["h_all"]
            wr = jnp.exp(tot_r - s["cs_rd"]) * s["dt_rd"]
            wr_big = jnp.concatenate([jnp.broadcast_to(wr[e:e + 1, :], (P, L)) for e in range(E)], axis=0)
            upd = jnp.dot((xst_scr[s["c"]] * wr_big).astype(BF16), s["bc"], preferred_element_type=F32)
            etot = jnp.exp(tot_r)
            for e in range(E):
                h_scr[d, e * P:(e + 1) * P, :] = (etot[e:e + 1, :] * h_all[e * P:(e + 1) * P, :]
                                                  + upd[e * P:(e + 1) * P, :])
        return carry

    lax.fori_loop(0, n_chunks, chunk, 0)
    if has_state_out:
        for d in range(2):
            for e in range(E):
                ho_ref[d, e] = h_scr[d, e * P:(e + 1) * P, :]

    y = yf_scr[...] + yb_scr[...] + dsk_ref[...] * xs_scr[...]
    yz = y * _silu(z_ref[...].astype(F32))
    ssq = jnp.sum(yz * yz, axis=1, keepdims=True)

    @pl.when(pl.program_id(1) == 0)
    def _():
        ssq_ref[...] = ssq

    @pl.when(pl.program_id(1) != 0)
    def _():
        ssq_ref[...] += ssq

    y_ref[...] = (yz * ng_ref[...]).astype(BF16)


def _ssd_core(zxbc, dt_raw, conv_w, conv_b, dt_bias, a_log, d_skip, norm_g, *,
              batch, seq, row_blk0, init=None, state_out=False, prev=None):
    m = zxbc.shape[0]
    G, P, N, L = SSD_GROUPS, SSD_HD, SSD_N, SSD_CHUNK
    di = d_skip.shape[0] * P
    heads = di // P
    E = heads // G
    gw = E * P
    nc = seq // L
    r0 = row_blk0 * seq
    dt4 = dt_raw[r0:r0 + batch * seq].reshape(batch, seq, 2, G, E)
    dtc = jnp.transpose(dt4, (0, 3, 1, 2, 4)).reshape(batch, G, seq, 2 * E)
    dtr = jnp.transpose(dt4.reshape(batch, nc, L, 2, G, E), (0, 4, 1, 3, 5, 2)).reshape(batch, G, nc, 2 * E, L)

    def per_group(v):
        v3 = jnp.transpose(v.reshape(2, G, E), (1, 0, 2)).reshape(G, 2 * E)
        return v3[:, None, :], v3[:, :, None]

    dbc, dbr = per_group(dt_bias)
    alc, alr = per_group(a_log)
    x_blk0 = di // gw
    b_blk0 = (2 * di) // N
    c_blk0 = (2 * di + G * N) // N
    in_specs = [
        pl.BlockSpec((seq, gw), lambda b, g: (row_blk0 + b, g)),
        pl.BlockSpec((seq, gw), lambda b, g: (row_blk0 + b, x_blk0 + g)),
        pl.BlockSpec((seq, N), lambda b, g: (row_blk0 + b, b_blk0 + g)),
        pl.BlockSpec((seq, N), lambda b, g: (row_blk0 + b, c_blk0 + g)),
        pl.BlockSpec((SSD_CONV_W, gw), lambda b, g: (0, g)),
        pl.BlockSpec((SSD_CONV_W, N), lambda b, g: (0, di // N + g)),
        pl.BlockSpec((SSD_CONV_W, N), lambda b, g: (0, di // N + G + g)),
        pl.BlockSpec((1, gw), lambda b, g: (0, g)),
        pl.BlockSpec((1, N), lambda b, g: (0, di // N + g)),
        pl.BlockSpec((1, N), lambda b, g: (0, di // N + G + g)),
        pl.BlockSpec((None, None, seq, 2 * E), lambda b, g: (b, g, 0, 0)),
        pl.BlockSpec((None, None, nc, 2 * E, L), lambda b, g: (b, g, 0, 0, 0)),
        pl.BlockSpec((None, 1, 2 * E), lambda b, g: (g, 0, 0)),
        pl.BlockSpec((None, 2 * E, 1), lambda b, g: (g, 0, 0)),
        pl.BlockSpec((None, 1, 2 * E), lambda b, g: (g, 0, 0)),
        pl.BlockSpec((None, 2 * E, 1), lambda b, g: (g, 0, 0)),
        pl.BlockSpec((1, gw), lambda b, g: (0, g)),
        pl.BlockSpec((1, gw), lambda b, g: (0, g)),
    ]
    cb2 = conv_b.reshape(1, -1)
    args = [zxbc, zxbc, zxbc, zxbc, conv_w, conv_w, conv_w, cb2, cb2, cb2,
            dtc, dtr, dbc, dbr, alc, alr,
            jnp.repeat(d_skip, P).reshape(1, di), norm_g.reshape(1, di)]
    if init is not None:
        in_specs.append(pl.BlockSpec((None, 2, E, P, N), lambda b, g: (b, 0, g, 0, 0)))
        args.append(init)
    out_specs = [
        pl.BlockSpec((seq, gw), lambda b, g: (row_blk0 + b, g)),
        pl.BlockSpec((seq, 1), lambda b, g: (row_blk0 + b, 0)),
    ]
    out_shape = [jax.ShapeDtypeStruct((m, di), BF16), jax.ShapeDtypeStruct((m, 1), F32)]
    if state_out:
        out_specs.append(pl.BlockSpec((None, 2, E, P, N), lambda b, g: (b, 0, g, 0, 0)))
        out_shape.append(jax.ShapeDtypeStruct((batch, 2, heads, P, N), F32))
    return _call_aliased(
        functools.partial(_ssd_kernel, n_chunks=nc, has_init=init is not None, has_state_out=state_out),
        {} if prev is None else {0: prev[0], 1: prev[1]},
        grid=(batch, G),
        in_specs=in_specs,
        args=args,
        out_specs=out_specs,
        out_shape=out_shape,
        scratch_shapes=[
            pltpu.VMEM((seq + 2 * CONV_PAD_ROWS, gw + 2 * N), BF16),
            pltpu.VMEM((seq, gw), F32),
            pltpu.VMEM((seq, gw), BF16),
            pltpu.VMEM((nc, gw, L), F32),
            pltpu.VMEM((seq, N), BF16),
            pltpu.VMEM((seq, N), BF16),
            pltpu.VMEM((seq, 2 * E), F32),
            pltpu.VMEM((seq, gw), F32),
            pltpu.VMEM((seq, gw), F32),
            pltpu.VMEM((2, E * P, N), F32),
        ],
        compiler_params=_params("parallel", "arbitrary"),
        name="ssd_core",
    )


def kernel(x_prompt, x_sample, cache_attn_k, cache_attn_v, state_mlstm_C, state_mlstm_n, state_mlstm_m, state_ssd_h, c, c_ctx, mod_w, mod_b, ln_g, ln_b, ffn_w_in, ffn_w_out, mlstm_w_in, mlstm_b_gate, mlstm_norm_g, mlstm_w_out, attn_w_qkv, attn_q_norm, attn_k_norm, attn_w_out, ssd_w_in, ssd_conv_w, ssd_conv_b, ssd_dt_bias, ssd_A_log, ssd_D, ssd_norm_g, ssd_w_out):
    bp, sp, d = x_prompt.shape
    bd, sd, _ = x_sample.shape
    depth = mod_w.shape[0]
    n_ctx = bp * sp
    m = n_ctx + bd * sd
    rows = _Rows(n_ctx, sd, d)
    alpha = (2 * depth) ** 0.25
    ctx_blk0, dec_blk0 = 0, n_ctx // sd

    cvec = jnp.zeros((N_SEG_PAD, d), F32).at[0].set(c_ctx).at[1:1 + bd].set(c)
    modv = _adaln(cvec, mod_w, mod_b).reshape(depth * N_SEG_PAD * N_MOD, 1, d)
    lnv = jnp.stack([ln_g, ln_b], axis=2).reshape(depth * 3 * 2, 1, d)

    x = jnp.concatenate([x_prompt.reshape(n_ctx, d), x_sample.reshape(bd * sd, d)], axis=0)
    h = _modulate(x, modv, rows, 0, 0, 1)

    qk_w, v_w = 2 * ML_HEADS * ML_DK, ML_HEADS * ML_DV
    mlstm_wt = jnp.swapaxes(mlstm_w_in, 1, 2)

    ml_states, attn_kv, ssd_states = None, [], []
    for i in range(depth):
        kind, j = i % 3, i // 3
        act = _ffn_in(h, ffn_w_in, (i, 0))
        x, h = _out_ln(act, ffn_w_out, (i, 0), x, modv, lnv, rows, layer=i, k_gate=2, ln_idx=3 * i,
                       alpha=alpha, coef=0.5, nxt=(i, 3, 4), name="ffn_out")
        ssq = None
        if kind == 0:
            qkv = _proj(h, mlstm_wt, (j,), 0, qk_w + v_w, BF16, scale_first=ML_DK ** -0.5,
                        bn=ML_HEADS * ML_DK, w_t=True, name="mlstm_qkv")
            og = _proj(h, mlstm_wt, (j,), qk_w + v_w, v_w, BF16, w_t=True, name="mlstm_o")
            n_gate = 4 * ML_HEADS
            gates = _small_proj(h, mlstm_wt, (j,), (qk_w + 2 * v_w) // n_gate, n_gate, w_t=True, name="mlstm_gates")
            y, *ml_states = _mlstm_core(qkv, og, gates, mlstm_b_gate[j], mlstm_norm_g[j],
                                        batch=bp, seq=sp, row_blk0=ctx_blk0,
                                        state_out=(mlstm_w_in.shape[0], j, ml_states))
            (y,) = _mlstm_core(qkv, og, gates, mlstm_b_gate[j], mlstm_norm_g[j],
                               batch=bd, seq=sd, row_blk0=dec_blk0, y_prev=y,
                               init=(state_mlstm_C, state_mlstm_n, state_mlstm_m, j))
            w_out = mlstm_w_out
        elif kind == 1:
            qkv = _proj(h, attn_w_qkv, (j,), 0, attn_w_qkv.shape[2], BF16, name="attn_qkv")
            y, kc, vc = _attn_core(qkv, attn_q_norm[j], attn_k_norm[j], batch=bp, seq=sp, row_blk0=ctx_blk0)
            past = cache_attn_k.shape[2]
            (y,) = _attn_core(qkv, attn_q_norm[j], attn_k_norm[j], batch=bd, seq=sd, row_blk0=dec_blk0,
                              rope=_rope_tables(sd), y_prev=y,
                              ctx=(cache_attn_k[:, j].reshape(bd, past, -1), cache_attn_v[:, j].reshape(bd, past, -1)))
            attn_kv.append((kc.reshape(bp, sp, ATTN_KV_HEADS, ATTN_HD), vc.reshape(bp, sp, ATTN_KV_HEADS, ATTN_HD)))
            w_out = attn_w_out
        else:
            di = ssd_D.shape[1] * SSD_HD
            zxbc_w = 2 * di + 2 * SSD_GROUPS * SSD_N
            zxbc = _proj(h, ssd_w_in, (j,), 0, zxbc_w, BF16, name="ssd_zxbc")
            dt_raw = _small_proj(h, ssd_w_in, (j,), zxbc_w // LANES, LANES, name="ssd_dt")
            y, ssq, hst = _ssd_core(zxbc, dt_raw, ssd_conv_w[j], ssd_conv_b[j], ssd_dt_bias[j], ssd_A_log[j],
                                    ssd_D[j], ssd_norm_g[j], batch=bp, seq=sp, row_blk0=ctx_blk0, state_out=True)
            y, ssq = _ssd_core(zxbc, dt_raw, ssd_conv_w[j], ssd_conv_b[j], ssd_dt_bias[j], ssd_A_log[j],
                               ssd_D[j], ssd_norm_g[j], batch=bd, seq=sd, row_blk0=dec_blk0,
                               init=state_ssd_h[:, j], prev=(y, ssq))
            ssd_states.append(hst)
            w_out = ssd_w_out
        x, h = _out_ln(y, w_out, (j,), x, modv, lnv, rows, layer=i, k_gate=5, ln_idx=3 * i + 1,
                       alpha=alpha, coef=1.0, nxt=(i, 6, 7), ssq=ssq,
                       ssq_dim=None if ssq is None else w_out.shape[1], name="mix_out")
        act = _ffn_in(h, ffn_w_in, (i, 1))
        nxt = (i + 1, 0, 1) if i + 1 < depth else None
        x, h = _out_ln(act, ffn_w_out, (i, 1), x, modv, lnv, rows, layer=i, k_gate=8, ln_idx=3 * i + 2,
                       alpha=alpha, coef=0.5, nxt=nxt, name="ffn_out")

    y_prompt = x[:n_ctx].reshape(bp, sp, d)
    y_sample = x[n_ctx:].reshape(bd, sd, d)
    new_k = jnp.stack([kv[0] for kv in attn_kv], axis=1)
    new_v = jnp.stack([kv[1] for kv in attn_kv], axis=1)
    new_c = ml_states[0]
    new_n = ml_states[1].reshape(ml_states[1].shape[:4] + (ML_DK,))
    new_m = ml_states[2].reshape(ml_states[2].shape[:4])
    new_h = jnp.stack(ssd_states, axis=1)
    return (y_prompt, y_sample, new_k, new_v, new_c, new_n, new_m, new_h)
```

```python
import functools
import math

import jax
import jax.numpy as jnp
from jax import lax
from jax.experimental import pallas as pl
from jax.experimental.pallas import tpu as pltpu

F32 = jnp.float32
BF16 = jnp.bfloat16

LN_EPS = 1e-6
N_MOD = 9
N_SEG_PAD = 8
LANES = 128
VMEM_LIMIT = 60 * 1024 * 1024
LN_ROWS = 256
LN_UNROLL = 1

ML_HEADS, ML_DK, ML_DV, ML_CHUNK = 8, 128, 256, 128
ML_HEADS_PER_STEP = 4
ATTN_HEADS, ATTN_KV_HEADS, ATTN_HD = 16, 4, 128
ATTN_GROUP = ATTN_HEADS // ATTN_KV_HEADS
ATTN_QBLOCK = 256
GRID_W = 64
ROPE_THETA = 10000.0
ROPE_PAIRS = ATTN_HD // 4
SSD_HD, SSD_GROUPS, SSD_N, SSD_CONV_W, SSD_CHUNK = 64, 8, 128, 5, 128
CONV_PAD_ROWS = 16


def _params(*sem):
    return pltpu.CompilerParams(dimension_semantics=sem, vmem_limit_bytes=VMEM_LIMIT)


def _lead_spec(lead, block, index_fn):
    lead = tuple(lead)
    return pl.BlockSpec((None,) * len(lead) + tuple(block), lambda *g: lead + tuple(index_fn(*g)))


def _call_aliased(kernel_fn, prev, *, in_specs, args, out_specs, out_shape, **kw):
    prev = dict(prev or {})
    n = len(prev)

    def body(*refs):
        return kernel_fn(*refs[n:])

    return pl.pallas_call(
        body,
        in_specs=[pl.BlockSpec(memory_space=pl.ANY)] * n + list(in_specs),
        out_specs=out_specs,
        out_shape=out_shape,
        input_output_aliases={i: o for i, o in enumerate(prev)},
        **kw,
    )(*prev.values(), *args)


def _silu(x):
    return x * jax.nn.sigmoid(x)


def _log_sigmoid(x):
    return jnp.minimum(x, 0.0) - jnp.log1p(jnp.exp(-jnp.abs(x)))


def _softplus(x):
    return jnp.maximum(x, 0.0) + jnp.log1p(jnp.exp(-jnp.abs(x)))


def _split_bf16(x, n):
    parts, r = [], x
    for _ in range(n):
        p = r.astype(BF16)
        parts.append(p)
        r = r - p.astype(F32)
    return parts


def _split3(x):
    return _split_bf16(x, 3)


def _split2(x):
    return _split_bf16(x, 2)


def _dot_split(parts, mask_bf, *, lhs_is_mask):
    acc = None
    for p in reversed(parts):
        t = (jnp.dot(mask_bf, p, preferred_element_type=F32) if lhs_is_mask
             else jnp.dot(p, mask_bf, preferred_element_type=F32))
        acc = t if acc is None else acc + t
    return acc


def _adaln_kernel(cv_ref, w_ref, b_ref, o_ref):
    s = _silu(cv_ref[...]).astype(BF16)
    w = w_ref[...].astype(BF16)
    o_ref[...] = jnp.dot(s, w, preferred_element_type=F32) + b_ref[...]


def _adaln(cvec, mod_w, mod_b, bn=1024):
    depth, d, n = mod_w.shape
    return pl.pallas_call(
        _adaln_kernel,
        grid=(depth, n // bn),
        in_specs=[
            pl.BlockSpec((N_SEG_PAD, d), lambda l, j: (0, 0)),
            pl.BlockSpec((None, d, bn), lambda l, j: (l, 0, j)),
            pl.BlockSpec((None, 1, bn), lambda l, j: (l, 0, j)),
        ],
        out_specs=pl.BlockSpec((None, N_SEG_PAD, bn), lambda l, j: (l, 0, j)),
        out_shape=jax.ShapeDtypeStruct((depth, N_SEG_PAD, n), F32),
        compiler_params=_params("arbitrary", "arbitrary"),
        name="adaln",
    )(cvec, mod_w, mod_b.reshape(depth, 1, n))


class _Rows:
    def __init__(self, n_ctx_rows, dec_seq, d_model):
        self.n_ctx = n_ctx_rows
        self.dec_seq = dec_seq
        self.d = d_model

    def seg(self, i, bm):
        r = i * bm
        return jnp.where(r < self.n_ctx, 0, 1 + (r - self.n_ctx) // self.dec_seq)


def _mod_spec(rows, layer, k, bm):
    def idx(i, *_):
        return ((layer * N_SEG_PAD + rows.seg(i, bm)) * N_MOD + k, 0, 0)

    return pl.BlockSpec((None, 1, rows.d), idx)


def _modulate_kernel(x_ref, sh_ref, sc_ref, h_ref):
    h_ref[...] = (x_ref[...] * (1.0 + sc_ref[...]) + sh_ref[...]).astype(BF16)


def _modulate(x, modv, rows, layer, k_shift, k_scale, bm=512):
    m, d = x.shape
    return pl.pallas_call(
        _modulate_kernel,
        grid=(m // bm,),
        in_specs=[
            pl.BlockSpec((bm, d), lambda i: (i, 0)),
            _mod_spec(rows, layer, k_shift, bm),
            _mod_spec(rows, layer, k_scale, bm),
        ],
        out_specs=pl.BlockSpec((bm, d), lambda i: (i, 0)),
        out_shape=jax.ShapeDtypeStruct((m, d), BF16),
        compiler_params=_params("parallel"),
        name="modulate",
    )(x, modv, modv)


def _ffn_in_kernel(a_ref, wg_ref, wu_ref, o_ref, w_scr):
    bn = wg_ref.shape[1]

    @pl.when(pl.program_id(1) == 0)
    def _():
        w_scr[:, :bn] = wg_ref[...].astype(BF16)
        w_scr[:, bn:] = wu_ref[...].astype(BF16)

    u = jnp.dot(a_ref[...], w_scr[...], preferred_element_type=F32)
    o_ref[...] = (_silu(u[:, :bn]) * u[:, bn:]).astype(BF16)


def _ffn_in(a, w_in, lead, bm=1024, bn=512):
    m, k = a.shape
    f = w_in.shape[-1] // 2
    nt = f // bn
    return pl.pallas_call(
        _ffn_in_kernel,
        grid=(nt, m // bm),
        in_specs=[
            pl.BlockSpec((bm, k), lambda j, i: (i, 0)),
            _lead_spec(lead, (k, bn), lambda j, i: (0, j)),
            _lead_spec(lead, (k, bn), lambda j, i: (0, j + nt)),
        ],
        out_specs=pl.BlockSpec((bm, bn), lambda j, i: (i, j)),
        out_shape=jax.ShapeDtypeStruct((m, f), BF16),
        scratch_shapes=[pltpu.VMEM((k, 2 * bn), BF16)],
        compiler_params=_params("arbitrary", "arbitrary"),
        name="ffn_in",
    )(a, w_in, w_in)


def _proj_kernel(a_ref, w_ref, o_ref, w_scr, *, scale_first, w_t):
    @pl.when(pl.program_id(1) == 0)
    def _():
        w = w_ref[...]
        w_scr[...] = (jnp.transpose(w) if w_t else w).astype(BF16)

    u = jnp.dot(a_ref[...], w_scr[...], preferred_element_type=F32)
    if scale_first is not None:
        u = u * jnp.where(pl.program_id(0) == 0, scale_first, 1.0)
    o_ref[...] = u.astype(o_ref.dtype)


def _proj(a, w, lead, col0, ncols, out_dtype, bm=1024, bn=1024, scale_first=None, w_t=False, name="proj"):
    m, k = a.shape
    assert col0 % bn == 0 and ncols % bn == 0
    j0 = col0 // bn
    w_spec = (_lead_spec(lead, (bn, k), lambda j, i: (j + j0, 0)) if w_t
              else _lead_spec(lead, (k, bn), lambda j, i: (0, j + j0)))
    return pl.pallas_call(
        functools.partial(_proj_kernel, scale_first=scale_first, w_t=w_t),
        grid=(ncols // bn, m // bm),
        in_specs=[
            pl.BlockSpec((bm, k), lambda j, i: (i, 0)),
            w_spec,
        ],
        out_specs=pl.BlockSpec((bm, bn), lambda j, i: (i, j)),
        out_shape=jax.ShapeDtypeStruct((m, ncols), out_dtype),
        scratch_shapes=[pltpu.VMEM((k, bn), BF16)],
        compiler_params=_params("arbitrary", "arbitrary"),
        name=name,
    )(a, w)


def _small_proj_kernel(a_ref, w_ref, o_ref, *, w_t):
    w = w_ref[...].astype(BF16)
    if w_t:
        o_ref[...] = lax.dot_general(a_ref[...], w, (((1,), (1,)), ((), ())), preferred_element_type=F32)
    else:
        o_ref[...] = jnp.dot(a_ref[...], w, preferred_element_type=F32)


def _small_proj(a, w, lead, col_block, ncols, bm=1024, w_t=False, name="small_proj"):
    m, k = a.shape
    w_spec = (_lead_spec(lead, (ncols, k), lambda i: (col_block, 0)) if w_t
              else _lead_spec(lead, (k, ncols), lambda i: (0, col_block)))
    return pl.pallas_call(
        functools.partial(_small_proj_kernel, w_t=w_t),
        grid=(m // bm,),
        in_specs=[
            pl.BlockSpec((bm, k), lambda i: (i, 0)),
            w_spec,
        ],
        out_specs=pl.BlockSpec((bm, ncols), lambda i: (i, 0)),
        out_shape=jax.ShapeDtypeStruct((m, ncols), F32),
        compiler_params=_params("parallel"),
        name=name,
    )(a, w)


def _out_ln_kernel(*refs, alpha, coef, has_next, has_ssq, ssq_dim):
    it = iter(refs)
    a_ref, w_ref, x_ref, gate_ref, g_ref, b_ref = (next(it) for _ in range(6))
    sh_ref = sc_ref = ssq_ref = None
    if has_next:
        sh_ref, sc_ref = next(it), next(it)
    if has_ssq:
        ssq_ref = next(it)
    xo_ref = next(it)
    ho_ref = next(it) if has_next else None

    k = pl.program_id(1)

    @pl.when(k == 0)
    def _():
        xo_ref[...] = jnp.dot(a_ref[...], w_ref[...].astype(BF16), preferred_element_type=F32)

    @pl.when(k != 0)
    def _():
        xo_ref[...] += jnp.dot(a_ref[...], w_ref[...].astype(BF16), preferred_element_type=F32)

    @pl.when(k == pl.num_programs(1) - 1)
    def _():
        d_model = xo_ref.shape[1]
        tiles = [slice(t * LANES, (t + 1) * LANES) for t in range(d_model // LANES)]

        def slab(si, carry):
            rs = pl.ds(pl.multiple_of(si * LN_ROWS, LN_ROWS), LN_ROWS)
            r = lax.rsqrt(ssq_ref[rs, :] * (1.0 / ssq_dim) + LN_EPS) if has_ssq else None
            acc = None
            for cs in tiles:
                y = xo_ref[rs, cs]
                if has_ssq:
                    y = y * r
                z = alpha * x_ref[rs, cs] + (coef * gate_ref[:, cs]) * y
                xo_ref[rs, cs] = z
                acc = z if acc is None else acc + z
            mu = jnp.sum(acc, axis=1, keepdims=True) * (1.0 / d_model)
            acc = None
            for cs in tiles:
                zc = xo_ref[rs, cs] - mu
                xo_ref[rs, cs] = zc
                acc = zc * zc if acc is None else acc + zc * zc
            rstd = lax.rsqrt(jnp.sum(acc, axis=1, keepdims=True) * (1.0 / d_model) + LN_EPS)
            for cs in tiles:
                xn = xo_ref[rs, cs] * rstd * g_ref[:, cs] + b_ref[:, cs]
                xo_ref[rs, cs] = xn
                if has_next:
                    ho_ref[rs, cs] = (xn * (1.0 + sc_ref[:, cs]) + sh_ref[:, cs]).astype(BF16)
            return carry

        lax.fori_loop(0, xo_ref.shape[0] // LN_ROWS, slab, 0, unroll=LN_UNROLL)


def _out_ln(a, w, lead, x, modv, lnv, rows, *, layer, k_gate, ln_idx, alpha, coef,
            nxt=None, ssq=None, ssq_dim=None, bm=1024, bk=512, name="out_ln"):
    m, kdim = a.shape
    d = x.shape[1]
    assert kdim % bk == 0
    in_specs = [
        pl.BlockSpec((bm, bk), lambda i, k: (i, k)),
        _lead_spec(lead, (bk, d), lambda i, k: (k, 0)),
        pl.BlockSpec((bm, d), lambda i, k: (i, 0)),
        _mod_spec(rows, layer, k_gate, bm),
        pl.BlockSpec((None, 1, d), lambda i, k: (2 * ln_idx, 0, 0)),
        pl.BlockSpec((None, 1, d), lambda i, k: (2 * ln_idx + 1, 0, 0)),
    ]
    args = [a, w, x, modv, lnv, lnv]
    if nxt is not None:
        in_specs += [_mod_spec(rows, nxt[0], nxt[1], bm), _mod_spec(rows, nxt[0], nxt[2], bm)]
        args += [modv, modv]
    if ssq is not None:
        in_specs.append(pl.BlockSpec((bm, ssq.shape[1]), lambda i, k: (i, 0)))
        args.append(ssq)
    out_specs = [pl.BlockSpec((bm, d), lambda i, k: (i, 0))]
    out_shape = [jax.ShapeDtypeStruct((m, d), F32)]
    if nxt is not None:
        out_specs.append(pl.BlockSpec((bm, d), lambda i, k: (i, 0)))
        out_shape.append(jax.ShapeDtypeStruct((m, d), BF16))
    res = pl.pallas_call(
        functools.partial(_out_ln_kernel, alpha=alpha, coef=coef, has_next=nxt is not None,
                          has_ssq=ssq is not None, ssq_dim=ssq_dim),
        grid=(m // bm, kdim // bk),
        in_specs=in_specs,
        out_specs=out_specs,
        out_shape=out_shape,
        compiler_params=_params("parallel", "arbitrary"),
        name=name,
    )(*args)
    return (res[0], res[1]) if nxt is not None else (res[0], None)


def _mlstm_kernel(*refs, n_chunks, hp, has_init, has_state_out):
    it = iter(refs)
    q_ref, k_ref, v_ref, o_ref, gr_ref, br_ref, ng_ref = (next(it) for _ in range(7))
    c0_ref = n0_ref = m0_ref = None
    if has_init:
        c0_ref, n0_ref, m0_ref = next(it), next(it), next(it)
    y_ref = next(it)
    co_ref = no_ref = mo_ref = None
    if has_state_out:
        co_ref, no_ref, mo_ref = next(it), next(it), next(it)
    hf_scr, hb_scr, kt_scr, vx_scr, cn_scr, m_scr = (next(it) for _ in range(6))

    L, dk, dv = ML_CHUNK, ML_DK, ML_DV
    t_idx = lax.broadcasted_iota(jnp.int32, (L, L), 0)
    s_idx = lax.broadcasted_iota(jnp.int32, (L, L), 1)
    ones_bf = jnp.ones((L, LANES), BF16)
    masks = []
    for d in range(2):
        incl = (s_idx <= t_idx) if d == 0 else (s_idx >= t_idx)
        incl_t = (t_idx <= s_idx) if d == 0 else (t_idx >= s_idx)
        masks.append((incl, jnp.where(incl_t, 1.0, 0.0).astype(BF16)))

    for hh in range(hp):
        vx_scr[:, hh * (dv + LANES):hh * (dv + LANES) + dv] = v_ref[:, hh * dv:(hh + 1) * dv]
        vx_scr[:, hh * (dv + LANES) + dv:(hh + 1) * (dv + LANES)] = jnp.ones((v_ref.shape[0], LANES), BF16)
        for c in range(n_chunks):
            kt_scr[c, hh * dk:(hh + 1) * dk, :] = jnp.transpose(
                k_ref[c * L:(c + 1) * L, hh * dk:(hh + 1) * dk].astype(F32))
        for d in range(2):
            i = hh * 2 + d
            if has_init:
                cn_scr[i, :, :dv] = c0_ref[d, hh]
                cn_scr[i, :, dv:] = jnp.transpose(jnp.broadcast_to(n0_ref[d, hh], (LANES, dk)))
                m_scr[i] = m0_ref[d, hh]
            else:
                cn_scr[i] = jnp.zeros((dk, dv + LANES), F32)
                m_scr[i] = jnp.zeros((1, 1), F32)

    def step(ci, carry):
        chains = []
        for d in range(2):
            incl, incl_t_bf = masks[d]
            c = ci if d == 0 else n_chunks - 1 - ci
            r0 = pl.multiple_of(c * L, L)
            last = L - 1 if d == 0 else 0
            grow = gr_ref[c] + br_ref[...]
            lf_parts = _split3(_log_sigmoid(grow))
            cum_r = _dot_split(lf_parts, incl_t_bf, lhs_is_mask=False)
            lf_parts = [p.astype(F32) for p in lf_parts]
            for hh in range(hp):
                gi, gf = hh * 4 + d, hh * 4 + 2 + d
                ch = dict(d=d, hh=hh, i=hh * 2 + d, c=c, r0=r0, incl=incl,
                          li_r=grow[gi:gi + 1, :], b_r=cum_r[gf:gf + 1, :],
                          lf_rows=[p[gf:gf + 1, :] for p in lf_parts])
                ch["g"] = ch["b_r"][:, last:last + 1]
                chains.append(ch)

        for ch in chains:
            b_c = None
            for p in reversed(ch["lf_rows"]):
                t = jnp.dot(jnp.where(ch["incl"], p, 0.0).astype(BF16), ones_bf, preferred_element_type=F32)
                b_c = t if b_c is None else b_c + t
            ch["b_c"] = b_c
        for ch in chains:
            hh, r0 = ch["hh"], ch["r0"]
            qc = q_ref[pl.ds(r0, L), hh * dk:(hh + 1) * dk]
            kc = k_ref[pl.ds(r0, L), hh * dk:(hh + 1) * dk]
            ch["cn_prev"] = cn_scr[ch["i"]]
            ch["m_prev"] = m_scr[ch["i"]]
            ch["qk"] = lax.dot_general(qc, kc, (((1,), (1,)), ((), ())), preferred_element_type=F32)
            ch["q_cn"] = jnp.dot(qc, ch["cn_prev"].astype(BF16), preferred_element_type=F32)
        for ch in chains:
            hh, r0 = ch["hh"], ch["r0"]
            dmat = jnp.where(ch["incl"], ch["b_c"] - ch["b_r"] + ch["li_r"], -jnp.inf)
            inter = ch["b_c"] + ch["m_prev"]
            m_t = jnp.maximum(inter, jnp.max(dmat, axis=1, keepdims=True))
            s_bf = (ch["qk"] * jnp.exp(dmat - m_t)).astype(BF16)
            ch["sc"] = jnp.exp(inter - m_t)
            ch["floor"] = jnp.exp(-m_t)
            ch["vc"] = vx_scr[pl.ds(r0, L), hh * (dv + LANES):(hh + 1) * (dv + LANES)]
            ch["s_v"] = jnp.dot(s_bf, ch["vc"], preferred_element_type=F32)
        for ch in chains:
            hh, r0, sc, q_cn, s_v = ch["hh"], ch["r0"], ch["sc"], ch["q_cn"], ch["s_v"]
            hs_scr = hf_scr if ch["d"] == 0 else hb_scr
            den = sc * q_cn[:, dv:] + s_v[:, dv:]
            inv = 1.0 / jnp.maximum(jnp.abs(den), ch["floor"])
            for cb in range(dv // LANES):
                cs = slice(cb * LANES, (cb + 1) * LANES)
                hs_scr[pl.ds(r0, L), hh * dv + cb * LANES:hh * dv + (cb + 1) * LANES] = (
                    sc * q_cn[:, cs] + s_v[:, cs]) * inv
        for ch in chains:
            hh, g, m_prev = ch["hh"], ch["g"], ch["m_prev"]
            ds_r = g - ch["b_r"] + ch["li_r"]
            m_new = jnp.maximum(g + m_prev, jnp.max(ds_r, axis=1, keepdims=True))
            ktw = (kt_scr[ch["c"], hh * dk:(hh + 1) * dk, :] * jnp.exp(ds_r - m_new)).astype(BF16)
            decay = jnp.exp(g + m_prev - m_new)
            cn_scr[ch["i"]] = decay * ch["cn_prev"] + jnp.dot(ktw, ch["vc"], preferred_element_type=F32)
            m_scr[ch["i"]] = m_new
        return carry

    lax.fori_loop(0, n_chunks, step, 0)

    for hh in range(hp):
        if has_state_out:
            for d in range(2):
                co_ref[d, hh] = cn_scr[hh * 2 + d, :, :dv]
                no_ref[d, hh] = jnp.transpose(cn_scr[hh * 2 + d, :, dv:])[0:1, :]
                mo_ref[d, hh] = m_scr[hh * 2 + d]
        cols = slice(hh * dv, (hh + 1) * dv)
        hs = hf_scr[:, cols] + hb_scr[:, cols]
        mu = jnp.mean(hs, axis=1, keepdims=True)
        hc = hs - mu
        var = jnp.mean(hc * hc, axis=1, keepdims=True)
        hn = hc * lax.rsqrt(var + LN_EPS) * ng_ref[:, cols]
        y_ref[:, cols] = (jax.nn.sigmoid(o_ref[:, cols].astype(F32)) * hn).astype(BF16)


def _mlstm_core(qkv, o, gates, b_gate, norm_g, *, batch, seq, row_blk0, init=None, state_out=None, y_prev=None):
    m = qkv.shape[0]
    h, dk, dv, L, hp = ML_HEADS, ML_DK, ML_DV, ML_CHUNK, ML_HEADS_PER_STEP
    hg = h // hp
    nc = seq // L
    r0 = row_blk0 * seq
    g4 = jnp.swapaxes(gates[r0:r0 + batch * seq].reshape(batch, seq, 4, h), 2, 3)
    grow = jnp.transpose(g4.reshape(batch, nc, L, hg, hp * 4), (0, 3, 1, 4, 2))
    brow = jnp.transpose(b_gate.reshape(4, h)).reshape(hg, hp * 4, 1)
    kq = (h * dk) // (hp * dk)
    kv = (2 * h * dk) // (hp * dv)
    in_specs = [
        pl.BlockSpec((seq, hp * dk), lambda b, g: (row_blk0 + b, g)),
        pl.BlockSpec((seq, hp * dk), lambda b, g: (row_blk0 + b, kq + g)),
        pl.BlockSpec((seq, hp * dv), lambda b, g: (row_blk0 + b, kv + g)),
        pl.BlockSpec((seq, hp * dv), lambda b, g: (row_blk0 + b, g)),
        pl.BlockSpec((None, None, nc, hp * 4, L), lambda b, g: (b, g, 0, 0, 0)),
        pl.BlockSpec((None, hp * 4, 1), lambda b, g: (g, 0, 0)),
        pl.BlockSpec((1, hp * dv), lambda b, g: (0, g)),
    ]
    args = [qkv, qkv, qkv, o, grow, brow, norm_g.reshape(1, h * dv)]
    if init is not None:
        c0, n0, m0, lyr = init
        nl = c0.shape[1]
        in_specs += [
            pl.BlockSpec((None, None, 2, hp, dk, dv), lambda b, g: (b, lyr, 0, g, 0, 0)),
            pl.BlockSpec((None, None, 2, hp, 1, dk), lambda b, g: (b, lyr, 0, g, 0, 0)),
            pl.BlockSpec((None, None, 2, hp, 1, 1), lambda b, g: (b, lyr, 0, g, 0, 0)),
        ]
        args += [c0, n0.reshape(batch, nl, 2, h, 1, dk), m0.reshape(batch, nl, 2, h, 1, 1)]
    out_specs = [pl.BlockSpec((seq, hp * dv), lambda b, g: (row_blk0 + b, g))]
    out_shape = [jax.ShapeDtypeStruct((m, h * dv), BF16)]
    prev = {} if y_prev is None else {0: y_prev}
    if state_out is not None:
        nl_out, slot, st_prev = state_out
        out_specs += [
            pl.BlockSpec((None, None, 2, hp, dk, dv), lambda b, g: (b, slot, 0, g, 0, 0)),
            pl.BlockSpec((None, None, 2, hp, 1, dk), lambda b, g: (b, slot, 0, g, 0, 0)),
            pl.BlockSpec((None, None, 2, hp, 1, 1), lambda b, g: (b, slot, 0, g, 0, 0)),
        ]
        out_shape += [
            jax.ShapeDtypeStruct((batch, nl_out, 2, h, dk, dv), F32),
            jax.ShapeDtypeStruct((batch, nl_out, 2, h, 1, dk), F32),
            jax.ShapeDtypeStruct((batch, nl_out, 2, h, 1, 1), F32),
        ]
        if st_prev is not None:
            prev.update({1: st_prev[0], 2: st_prev[1], 3: st_prev[2]})
    return _call_aliased(
        functools.partial(_mlstm_kernel, n_chunks=nc, hp=hp, has_init=init is not None,
                          has_state_out=state_out is not None),
        prev,
        grid=(batch, hg),
        in_specs=in_specs,
        args=args,
        out_specs=out_specs,
        out_shape=out_shape,
        scratch_shapes=[
            pltpu.VMEM((seq, hp * dv), F32),
            pltpu.VMEM((seq, hp * dv), F32),
            pltpu.VMEM((nc, hp * dk, L), F32),
            pltpu.VMEM((seq, hp * (dv + LANES)), BF16),
            pltpu.VMEM((2 * hp, dk, dv + LANES), F32),
            pltpu.VMEM((2 * hp, 1, 1), F32),
        ],
        compiler_params=_params("parallel", "parallel"),
        name="mlstm_core",
    )


def _rope(x, cos_t, sin_t):
    lane = lax.broadcasted_iota(jnp.int32, x.shape, 1)
    first = (lane % (2 * ROPE_PAIRS)) < ROPE_PAIRS
    rot = jnp.where(first, pltpu.roll(x, x.shape[1] - ROPE_PAIRS, 1), pltpu.roll(x, ROPE_PAIRS, 1))
    return x * cos_t + rot * sin_t


def _rms(x, g):
    return x * lax.rsqrt(jnp.mean(x * x, axis=1, keepdims=True) + LN_EPS) * g


def _attn_kernel(*refs, seq, has_ctx):
    it = iter(refs)
    q_ref, k_ref, v_ref, qg_ref, kg_ref = (next(it) for _ in range(5))
    cos_ref = sin_ref = ck_ref = cv_ref = None
    if has_ctx:
        cos_ref, sin_ref, ck_ref, cv_ref = (next(it) for _ in range(4))
    o_ref = next(it)
    ko_ref = vo_ref = None
    if not has_ctx:
        ko_ref, vo_ref = next(it), next(it)
    keys_scr, vals_scr = next(it), next(it)

    hd = ATTN_HD
    kn = _rms(k_ref[...].astype(F32), kg_ref[...])
    if has_ctx:
        keys_scr[:seq, :] = _rope(kn, cos_ref[...], sin_ref[...]).astype(BF16)
        keys_scr[seq:, :] = ck_ref[...].astype(BF16)
        vals_scr[:seq, :] = v_ref[...]
        vals_scr[seq:, :] = cv_ref[...].astype(BF16)
    else:
        ko_ref[...] = kn
        vo_ref[...] = v_ref[...].astype(F32)
        keys_scr[...] = kn.astype(BF16)
        vals_scr[...] = v_ref[...]

    qb = min(ATTN_QBLOCK, seq)
    scale = hd ** -0.5

    def qblock(bi, carry):
        r0 = pl.multiple_of(bi * qb, qb)
        ss = []
        for g in range(ATTN_GROUP):
            q = _rms(q_ref[pl.ds(r0, qb), g * hd:(g + 1) * hd].astype(F32), qg_ref[...])
            if has_ctx:
                q = _rope(q, cos_ref[pl.ds(r0, qb), :], sin_ref[pl.ds(r0, qb), :])
            q = (q * scale).astype(BF16)
            ss.append(lax.dot_general(q, keys_scr[...], (((1,), (1,)), ((), ())), preferred_element_type=F32))
        ps, dens = [], []
        for s in ss:
            p = jnp.exp(s - jnp.max(s, axis=1, keepdims=True))
            dens.append(jnp.sum(p, axis=1, keepdims=True))
            ps.append(p.astype(BF16))
        for g in range(ATTN_GROUP):
            o = jnp.dot(ps[g], vals_scr[...], preferred_element_type=F32) / dens[g]
            o_ref[pl.ds(r0, qb), g * hd:(g + 1) * hd] = o.astype(BF16)
        return carry

    lax.fori_loop(0, seq // qb, qblock, 0)


def _attn_core(qkv, q_g, k_g, *, batch, seq, row_blk0, rope=None, ctx=None, y_prev=None):
    m = qkv.shape[0]
    hd, kvh, grp = ATTN_HD, ATTN_KV_HEADS, ATTN_GROUP
    has_ctx = ctx is not None
    k_off = ATTN_HEADS
    v_off = ATTN_HEADS + kvh
    in_specs = [
        pl.BlockSpec((seq, grp * hd), lambda b, kv: (row_blk0 + b, kv)),
        pl.BlockSpec((seq, hd), lambda b, kv: (row_blk0 + b, k_off + kv)),
        pl.BlockSpec((seq, hd), lambda b, kv: (row_blk0 + b, v_off + kv)),
        pl.BlockSpec((1, hd), lambda b, kv: (0, 0)),
        pl.BlockSpec((1, hd), lambda b, kv: (0, 0)),
    ]
    args = [qkv, qkv, qkv, q_g.reshape(1, hd), k_g.reshape(1, hd)]
    n_keys = seq
    if has_ctx:
        cos_t, sin_t = rope
        ck, cv = ctx
        past = ck.shape[1]
        n_keys = seq + past
        in_specs += [
            pl.BlockSpec((seq, hd), lambda b, kv: (0, 0)),
            pl.BlockSpec((seq, hd), lambda b, kv: (0, 0)),
            pl.BlockSpec((None, past, hd), lambda b, kv: (b, 0, kv)),
            pl.BlockSpec((None, past, hd), lambda b, kv: (b, 0, kv)),
        ]
        args += [cos_t, sin_t, ck, cv]
    out_specs = [pl.BlockSpec((seq, grp * hd), lambda b, kv: (row_blk0 + b, kv))]
    out_shape = [jax.ShapeDtypeStruct((m, ATTN_HEADS * hd), BF16)]
    if not has_ctx:
        out_specs += [pl.BlockSpec((None, seq, hd), lambda b, kv: (b, 0, kv))] * 2
        out_shape += [jax.ShapeDtypeStruct((batch, seq, kvh * hd), F32)] * 2
    return _call_aliased(
        functools.partial(_attn_kernel, seq=seq, has_ctx=has_ctx),
        {} if y_prev is None else {0: y_prev},
        grid=(batch, kvh),
        in_specs=in_specs,
        args=args,
        out_specs=out_specs,
        out_shape=out_shape,
        scratch_shapes=[pltpu.VMEM((n_keys, hd), BF16), pltpu.VMEM((n_keys, hd), BF16)],
        compiler_params=_params("parallel", "parallel"),
        name="attn_core",
    )


def _rope_tables(seq):
    rows = seq // GRID_W
    row = jnp.repeat(jnp.arange(rows, dtype=F32), GRID_W)
    col = jnp.tile(jnp.arange(GRID_W, dtype=F32), rows)
    freqs = ROPE_THETA ** (-jnp.arange(ROPE_PAIRS, dtype=F32) / ROPE_PAIRS)
    ar, ac = row[:, None] * freqs, col[:, None] * freqs
    cos_t = jnp.concatenate([jnp.cos(ar), jnp.cos(ar), jnp.cos(ac), jnp.cos(ac)], axis=1)
    sin_t = jnp.concatenate([-jnp.sin(ar), jnp.sin(ar), -jnp.sin(ac), jnp.sin(ac)], axis=1)
    return cos_t, sin_t


def _dwconv_silu_chunk(xpad_ref, r0, w, b):
    L, padr = SSD_CHUNK, CONV_PAD_ROWS
    win = xpad_ref[pl.ds(r0, L + 2 * padr), :]
    t_idx = lax.broadcasted_iota(jnp.int32, (L, L + 2 * padr), 0)
    s_idx = lax.broadcasted_iota(jnp.int32, (L, L + 2 * padr), 1)
    acc = b
    for j in range(SSD_CONV_W):
        off = j - SSD_CONV_W // 2
        if off == 0:
            xs = win[padr:padr + L, :].astype(F32)
        else:
            shift = jnp.where(s_idx == t_idx + (padr + off), 1.0, 0.0).astype(BF16)
            xs = jnp.dot(shift, win, preferred_element_type=F32)
        acc = acc + xs * w[j:j + 1, :]
    return _silu(acc)


def _ssd_kernel(*refs, n_chunks, has_init, has_state_out):
    it = iter(refs)
    (z_ref, x_ref, b_ref, c_ref, wx_ref, wb_ref, wc_ref, bx_ref, bb_ref, bc_ref,
     dtc_ref, dtr_ref, dbc_ref, dbr_ref, alc_ref, alr_ref, dsk_ref, ng_ref) = (next(it) for _ in range(18))
    h0_ref = next(it) if has_init else None
    y_ref, ssq_ref = next(it), next(it)
    ho_ref = next(it) if has_state_out else None
    (xpad_scr, xs_scr, xsb_scr, xst_scr, bm_scr, cm_scr, dtc_scr,
     yf_scr, yb_scr, h_scr) = (next(it) for _ in range(10))

    L, P, N = SSD_CHUNK, SSD_HD, SSD_N
    gw = x_ref.shape[1]
    E = gw // P
    n_pairs = gw // LANES
    seq, padr = x_ref.shape[0], CONV_PAD_ROWS

    cw = gw + 2 * N
    xpad_scr[0:padr, :] = jnp.zeros((padr, cw), BF16)
    xpad_scr[padr + seq:, :] = jnp.zeros((padr, cw), BF16)
    xpad_scr[padr:padr + seq, 0:gw] = x_ref[...]
    xpad_scr[padr:padr + seq, gw:gw + N] = b_ref[...]
    xpad_scr[padr:padr + seq, gw + N:] = c_ref[...]
    conv_w = jnp.concatenate([wx_ref[...], wb_ref[...], wc_ref[...]], axis=1)
    conv_b = jnp.concatenate([bx_ref[...], bb_ref[...], bc_ref[...]], axis=1)
    for c in range(n_chunks):
        rows = slice(c * L, (c + 1) * L)
        xbc = _dwconv_silu_chunk(xpad_scr, c * L, conv_w, conv_b)
        xs_scr[rows, :] = xbc[:, :gw]
        xsb_scr[rows, :] = xbc[:, :gw].astype(BF16)
        bm_scr[rows, :] = xbc[:, gw:gw + N].astype(BF16)
        cm_scr[rows, :] = xbc[:, gw + N:].astype(BF16)
        for j in range(n_pairs):
            xst_scr[c, j * LANES:(j + 1) * LANES, :] = jnp.transpose(xbc[:, j * LANES:(j + 1) * LANES])
    dtc_scr[...] = _softplus(dtc_ref[...] + dbc_ref[...])
    a_c = -jnp.exp(alc_ref[...])
    a_r = -jnp.exp(alr_ref[...])

    t_idx = lax.broadcasted_iota(jnp.int32, (L, L), 0)
    s_idx = lax.broadcasted_iota(jnp.int32, (L, L), 1)
    lane_lo = lax.broadcasted_iota(jnp.int32, (L, LANES), 1) < P
    ex_row = lax.broadcasted_iota(jnp.int32, (2 * E, gw), 0)
    ex_head = jnp.right_shift(lax.broadcasted_iota(jnp.int32, (2 * E, gw), 1), P.bit_length() - 1)

    consts = []
    for d in range(2):
        incl = (s_idx <= t_idx) if d == 0 else (s_idx >= t_idx)
        incl_bf = jnp.where(incl, 1.0, 0.0).astype(BF16)
        incl_t_bf = jnp.where((t_idx <= s_idx) if d == 0 else (t_idx >= s_idx), 1.0, 0.0).astype(BF16)
        expand_bf = jnp.where(ex_row == ex_head + d * E, 1.0, 0.0).astype(BF16)
        consts.append((incl, incl_bf, incl_t_bf, expand_bf))
        if has_init:
            for e in range(E):
                h_scr[d, e * P:(e + 1) * P, :] = h0_ref[d, e]
        else:
            h_scr[d] = jnp.zeros((E * P, N), F32)

    def chunk(ci, carry):
        sd = []
        for d in range(2):
            incl, incl_bf, incl_t_bf, expand_bf = consts[d]
            last = L - 1 if d == 0 else 0
            c = ci if d == 0 else n_chunks - 1 - ci
            r0 = pl.multiple_of(c * L, L)
            s = dict(d=d, c=c, r0=r0, incl=incl, bc=bm_scr[pl.ds(r0, L), :], cc=cm_scr[pl.ds(r0, L), :],
                     h_all=h_scr[d])
            dt_r = _softplus(dtr_ref[c] + dbr_ref[...])
            da_c = dtc_scr[pl.ds(r0, L), :] * a_c
            da_r = dt_r * a_r
            s["cs_c"] = _dot_split(_split3(da_c), incl_bf, lhs_is_mask=True)
            cs_r = _dot_split(_split3(da_r), incl_t_bf, lhs_is_mask=False)
            s["cs_rd"] = cs_r[d * E:(d + 1) * E, :]
            s["dt_rd"] = dt_r[d * E:(d + 1) * E, :]
            s["tot_r"] = s["cs_rd"][:, last:last + 1]
            s["cb"] = lax.dot_general(s["cc"], s["bc"], (((1,), (1,)), ((), ())), preferred_element_type=F32)
            s["y_off"] = lax.dot_general(s["cc"], s["h_all"].astype(BF16), (((1,), (1,)), ((), ())),
                                         preferred_element_type=F32)
            s["expand_bf"] = expand_bf
            sd.append(s)
        for s in sd:
            s["ecs"] = _dot_split(_split2(jnp.exp(s["cs_c"])), s["expand_bf"], lhs_is_mask=False)
        for s in sd:
            d = s["d"]
            s["m"] = []
            for e in range(E):
                seg = s["cs_c"][:, d * E + e:d * E + e + 1] - s["cs_rd"][e:e + 1, :]
                m_e = s["cb"] * jnp.exp(jnp.where(s["incl"], seg, -jnp.inf)) * s["dt_rd"][e:e + 1, :]
                s["m"].append(m_e.astype(BF16))
        for s in sd:
            ys = []
            for j in range(n_pairs):
                xp = xsb_scr[pl.ds(s["r0"], L), j * LANES:(j + 1) * LANES]
                outs = [jnp.dot(s["m"][j * (LANES // P) + q], xp, preferred_element_type=F32)
                        for q in range(LANES // P)]
                ys.append(jnp.where(lane_lo, outs[0], outs[1]))
            y_scr = yf_scr if s["d"] == 0 else yb_scr
            y_scr[pl.ds(s["r0"], L), :] = jnp.concatenate(ys, axis=1) + s["ecs"] * s["y_off"]
        for s in sd:
            d, tot_r, h_all = s["d"], s["tot_r"], s["h_all"]
            wr = jnp.exp(tot_r - s["cs_rd"]) * s["dt_rd"]
            wr_big = jnp.concatenate([jnp.broadcast_to(wr[e:e + 1, :], (P, L)) for e in range(E)], axis=0)
            upd = jnp.dot((xst_scr[s["c"]] * wr_big).astype(BF16), s["bc"], preferred_element_type=F32)
            etot = jnp.exp(tot_r)
            for e in range(E):
                h_scr[d, e * P:(e + 1) * P, :] = (etot[e:e + 1, :] * h_all[e * P:(e + 1) * P, :]
                                                  + upd[e * P:(e + 1) * P, :])
        return carry

    lax.fori_loop(0, n_chunks, chunk, 0)
    if has_state_out:
        for d in range(2):
            for e in range(E):
                ho_ref[d, e] = h_scr[d, e * P:(e + 1) * P, :]

    y = yf_scr[...] + yb_scr[...] + dsk_ref[...] * xs_scr[...]
    yz = y * _silu(z_ref[...].astype(F32))
    ssq = jnp.sum(yz * yz, axis=1, keepdims=True)

    @pl.when(pl.program_id(1) == 0)
    def _():
        ssq_ref[...] = ssq

    @pl.when(pl.program_id(1) != 0)
    def _():
        ssq_ref[...] += ssq

    y_ref[...] = (yz * ng_ref[...]).astype(BF16)


def _ssd_core(zxbc, dt_raw, conv_w, conv_b, dt_bias, a_log, d_skip, norm_g, *,
              batch, seq, row_blk0, init=None, state_out=False, prev=None):
    m = zxbc.shape[0]
    G, P, N, L = SSD_GROUPS, SSD_HD, SSD_N, SSD_CHUNK
    di = d_skip.shape[0] * P
    heads = di // P
    E = heads // G
    gw = E * P
    nc = seq // L
    r0 = row_blk0 * seq
    dt4 = dt_raw[r0:r0 + batch * seq].reshape(batch, seq, 2, G, E)
    dtc = jnp.transpose(dt4, (0, 3, 1, 2, 4)).reshape(batch, G, seq, 2 * E)
    dtr = jnp.transpose(dt4.reshape(batch, nc, L, 2, G, E), (0, 4, 1, 3, 5, 2)).reshape(batch, G, nc, 2 * E, L)

    def per_group(v):
        v3 = jnp.transpose(v.reshape(2, G, E), (1, 0, 2)).reshape(G, 2 * E)
        return v3[:, None, :], v3[:, :, None]

    dbc, dbr = per_group(dt_bias)
    alc, alr = per_group(a_log)
    x_blk0 = di // gw
    b_blk0 = (2 * di) // N
    c_blk0 = (2 * di + G * N) // N
    in_specs = [
        pl.BlockSpec((seq, gw), lambda b, g: (row_blk0 + b, g)),
        pl.BlockSpec((seq, gw), lambda b, g: (row_blk0 + b, x_blk0 + g)),
        pl.BlockSpec((seq, N), lambda b, g: (row_blk0 + b, b_blk0 + g)),
        pl.BlockSpec((seq, N), lambda b, g: (row_blk0 + b, c_blk0 + g)),
        pl.BlockSpec((SSD_CONV_W, gw), lambda b, g: (0, g)),
        pl.BlockSpec((SSD_CONV_W, N), lambda b, g: (0, di // N + g)),
        pl.BlockSpec((SSD_CONV_W, N), lambda b, g: (0, di // N + G + g)),
        pl.BlockSpec((1, gw), lambda b, g: (0, g)),
        pl.BlockSpec((1, N), lambda b, g: (0, di // N + g)),
        pl.BlockSpec((1, N), lambda b, g: (0, di // N + G + g)),
        pl.BlockSpec((None, None, seq, 2 * E), lambda b, g: (b, g, 0, 0)),
        pl.BlockSpec((None, None, nc, 2 * E, L), lambda b, g: (b, g, 0, 0, 0)),
        pl.BlockSpec((None, 1, 2 * E), lambda b, g: (g, 0, 0)),
        pl.BlockSpec((None, 2 * E, 1), lambda b, g: (g, 0, 0)),
        pl.BlockSpec((None, 1, 2 * E), lambda b, g: (g, 0, 0)),
        pl.BlockSpec((None, 2 * E, 1), lambda b, g: (g, 0, 0)),
        pl.BlockSpec((1, gw), lambda b, g: (0, g)),
        pl.BlockSpec((1, gw), lambda b, g: (0, g)),
    ]
    cb2 = conv_b.reshape(1, -1)
    args = [zxbc, zxbc, zxbc, zxbc, conv_w, conv_w, conv_w, cb2, cb2, cb2,
            dtc, dtr, dbc, dbr, alc, alr,
            jnp.repeat(d_skip, P).reshape(1, di), norm_g.reshape(1, di)]
    if init is not None:
        in_specs.append(pl.BlockSpec((None, 2, E, P, N), lambda b, g: (b, 0, g, 0, 0)))
        args.append(init)
    out_specs = [
        pl.BlockSpec((seq, gw), lambda b, g: (row_blk0 + b, g)),
        pl.BlockSpec((seq, 1), lambda b, g: (row_blk0 + b, 0)),
    ]
    out_shape = [jax.ShapeDtypeStruct((m, di), BF16), jax.ShapeDtypeStruct((m, 1), F32)]
    if state_out:
        out_specs.append(pl.BlockSpec((None, 2, E, P, N), lambda b, g: (b, 0, g, 0, 0)))
        out_shape.append(jax.ShapeDtypeStruct((batch, 2, heads, P, N), F32))
    return _call_aliased(
        functools.partial(_ssd_kernel, n_chunks=nc, has_init=init is not None, has_state_out=state_out),
        {} if prev is None else {0: prev[0], 1: prev[1]},
        grid=(batch, G),
        in_specs=in_specs,
        args=args,
        out_specs=out_specs,
        out_shape=out_shape,
        scratch_shapes=[
            pltpu.VMEM((seq + 2 * CONV_PAD_ROWS, gw + 2 * N), BF16),
            pltpu.VMEM((seq, gw), F32),
            pltpu.VMEM((seq, gw), BF16),
            pltpu.VMEM((nc, gw, L), F32),
            pltpu.VMEM((seq, N), BF16),
            pltpu.VMEM((seq, N), BF16),
            pltpu.VMEM((seq, 2 * E), F32),
            pltpu.VMEM((seq, gw), F32),
            pltpu.VMEM((seq, gw), F32),
            pltpu.VMEM((2, E * P, N), F32),
        ],
        compiler_params=_params("parallel", "arbitrary"),
        name="ssd_core",
    )


def kernel(x_prompt, x_sample, cache_attn_k, cache_attn_v, state_mlstm_C, state_mlstm_n, state_mlstm_m, state_ssd_h, c, c_ctx, mod_w, mod_b, ln_g, ln_b, ffn_w_in, ffn_w_out, mlstm_w_in, mlstm_b_gate, mlstm_norm_g, mlstm_w_out, attn_w_qkv, attn_q_norm, attn_k_norm, attn_w_out, ssd_w_in, ssd_conv_w, ssd_conv_b, ssd_dt_bias, ssd_A_log, ssd_D, ssd_norm_g, ssd_w_out):
    bp, sp, d = x_prompt.shape
    bd, sd, _ = x_sample.shape
    depth = mod_w.shape[0]
    n_ctx = bp * sp
    m = n_ctx + bd * sd
    rows = _Rows(n_ctx, sd, d)
    alpha = (2 * depth) ** 0.25
    ctx_blk0, dec_blk0 = 0, n_ctx // sd

    cvec = jnp.zeros((N_SEG_PAD, d), F32).at[0].set(c_ctx).at[1:1 + bd].set(c)
    modv = _adaln(cvec, mod_w, mod_b).reshape(depth * N_SEG_PAD * N_MOD, 1, d)
    lnv = jnp.stack([ln_g, ln_b], axis=2).reshape(depth * 3 * 2, 1, d)

    x = jnp.concatenate([x_prompt.reshape(n_ctx, d), x_sample.reshape(bd * sd, d)], axis=0)
    h = _modulate(x, modv, rows, 0, 0, 1)

    qk_w, v_w = 2 * ML_HEADS * ML_DK, ML_HEADS * ML_DV
    mlstm_wt = jnp.swapaxes(mlstm_w_in, 1, 2)

    ml_states, attn_kv, ssd_states = None, [], []
    for i in range(depth):
        kind, j = i % 3, i // 3
        act = _ffn_in(h, ffn_w_in, (i, 0))
        x, h = _out_ln(act, ffn_w_out, (i, 0), x, modv, lnv, rows, layer=i, k_gate=2, ln_idx=3 * i,
                       alpha=alpha, coef=0.5, nxt=(i, 3, 4), name="ffn_out")
        ssq = None
        if kind == 0:
            qkv = _proj(h, mlstm_wt, (j,), 0, qk_w + v_w, BF16, scale_first=ML_DK ** -0.5,
                        bn=ML_HEADS * ML_DK, w_t=True, name="mlstm_qkv")
            og = _proj(h, mlstm_wt, (j,), qk_w + v_w, v_w, BF16, w_t=True, name="mlstm_o")
            n_gate = 4 * ML_HEADS
            gates = _small_proj(h, mlstm_wt, (j,), (qk_w + 2 * v_w) // n_gate, n_gate, w_t=True, name="mlstm_gates")
            y, *ml_states = _mlstm_core(qkv, og, gates, mlstm_b_gate[j], mlstm_norm_g[j],
                                        batch=bp, seq=sp, row_blk0=ctx_blk0,
                                        state_out=(mlstm_w_in.shape[0], j, ml_states))
            (y,) = _mlstm_core(qkv, og, gates, mlstm_b_gate[j], mlstm_norm_g[j],
                               batch=bd, seq=sd, row_blk0=dec_blk0, y_prev=y,
                               init=(state_mlstm_C, state_mlstm_n, state_mlstm_m, j))
            w_out = mlstm_w_out
        elif kind == 1:
            qkv = _proj(h, attn_w_qkv, (j,), 0, attn_w_qkv.shape[2], BF16, name="attn_qkv")
            y, kc, vc = _attn_core(qkv, attn_q_norm[j], attn_k_norm[j], batch=bp, seq=sp, row_blk0=ctx_blk0)
            past = cache_attn_k.shape[2]
            (y,) = _attn_core(qkv, attn_q_norm[j], attn_k_norm[j], batch=bd, seq=sd, row_blk0=dec_blk0,
                              rope=_rope_tables(sd), y_prev=y,
                              ctx=(cache_attn_k[:, j].reshape(bd, past, -1), cache_attn_v[:, j].reshape(bd, past, -1)))
            attn_kv.append((kc.reshape(bp, sp, ATTN_KV_HEADS, ATTN_HD), vc.reshape(bp, sp, ATTN_KV_HEADS, ATTN_HD)))
            w_out = attn_w_out
        else:
            di = ssd_D.shape[1] * SSD_HD
            zxbc_w = 2 * di + 2 * SSD_GROUPS * SSD_N
            zxbc = _proj(h, ssd_w_in, (j,), 0, zxbc_w, BF16, name="ssd_zxbc")
            dt_raw = _small_proj(h, ssd_w_in, (j,), zxbc_w // LANES, LANES, name="ssd_dt")
            y, ssq, hst = _ssd_core(zxbc, dt_raw, ssd_conv_w[j], ssd_conv_b[j], ssd_dt_bias[j], ssd_A_log[j],
                                    ssd_D[j], ssd_norm_g[j], batch=bp, seq=sp, row_blk0=ctx_blk0, state_out=True)
            y, ssq = _ssd_core(zxbc, dt_raw, ssd_conv_w[j], ssd_conv_b[j], ssd_dt_bias[j], ssd_A_log[j],
                               ssd_D[j], ssd_norm_g[j], batch=bd, seq=sd, row_blk0=dec_blk0,
                               init=state_ssd_h[:, j], prev=(y, ssq))
            ssd_states.append(hst)
            w_out = ssd_w_out
        x, h = _out_ln(y, w_out, (j,), x, modv, lnv, rows, layer=i, k_gate=5, ln_idx=3 * i + 1,
                       alpha=alpha, coef=1.0, nxt=(i, 6, 7), ssq=ssq,
                       ssq_dim=None if ssq is None else w_out.shape[1], name="mix_out")
        act = _ffn_in(h, ffn_w_in, (i, 1))
        nxt = (i + 1, 0, 1) if i + 1 < depth else None
        x, h = _out_ln(act, ffn_w_out, (i, 1), x, modv, lnv, rows, layer=i, k_gate=8, ln_idx=3 * i + 2,
                       alpha=alpha, coef=0.5, nxt=nxt, name="ffn_out")

    y_prompt = x[:n_ctx].reshape(bp, sp, d)
    y_sample = x[n_ctx:].reshape(bd, sd, d)
    new_k = jnp.stack([kv[0] for kv in attn_kv], axis=1)
    new_v = jnp.stack([kv[1] for kv in attn_kv], axis=1)
    new_c = ml_states[0]
    new_n = ml_states[1].reshape(ml_states[1].shape[:4] + (ML_DK,))
    new_m = ml_states[2].reshape(ml_states[2].shape[:4])
    new_h = jnp.stack(ssd_states, axis=1)
    return (y_prompt, y_sample, new_k, new_v, new_c, new_n, new_m, new_h)
```

```python
import functools
import math

import jax
import jax.numpy as jnp
from jax import lax
from jax.experimental import pallas as pl
from jax.experimental.pallas import tpu as pltpu

F32 = jnp.float32
BF16 = jnp.bfloat16

LN_EPS = 1e-6
N_MOD = 9
N_SEG_PAD = 8
LANES = 128
VMEM_LIMIT = 60 * 1024 * 1024
LN_ROWS = 256
LN_UNROLL = 1

ML_HEADS, ML_DK, ML_DV, ML_CHUNK = 8, 128, 256, 128
ML_HEADS_PER_STEP = 4
ATTN_HEADS, ATTN_KV_HEADS, ATTN_HD = 16, 4, 128
ATTN_GROUP = ATTN_HEADS // ATTN_KV_HEADS
ATTN_QBLOCK = 256
GRID_W = 64
ROPE_THETA = 10000.0
ROPE_PAIRS = ATTN_HD // 4
SSD_HD, SSD_GROUPS, SSD_N, SSD_CONV_W, SSD_CHUNK = 64, 8, 128, 5, 128
CONV_PAD_ROWS = 16


def _params(*sem):
    return pltpu.CompilerParams(dimension_semantics=sem, vmem_limit_bytes=VMEM_LIMIT)


def _lead_spec(lead, block, index_fn):
    lead = tuple(lead)
    return pl.BlockSpec((None,) * len(lead) + tuple(block), lambda *g: lead + tuple(index_fn(*g)))


def _call_aliased(kernel_fn, prev, *, in_specs, args, out_specs, out_shape, **kw):
    prev = dict(prev or {})
    n = len(prev)

    def body(*refs):
        return kernel_fn(*refs[n:])

    return pl.pallas_call(
        body,
        in_specs=[pl.BlockSpec(memory_space=pl.ANY)] * n + list(in_specs),
        out_specs=out_specs,
        out_shape=out_shape,
        input_output_aliases={i: o for i, o in enumerate(prev)},
        **kw,
    )(*prev.values(), *args)


def _silu(x):
    return x * jax.nn.sigmoid(x)


def _log_sigmoid(x):
    return jnp.minimum(x, 0.0) - jnp.log1p(jnp.exp(-jnp.abs(x)))


def _softplus(x):
    return jnp.maximum(x, 0.0) + jnp.log1p(jnp.exp(-jnp.abs(x)))


def _split_bf16(x, n):
    parts, r = [], x
    for _ in range(n):
        p = r.astype(BF16)
        parts.append(p)
        r = r - p.astype(F32)
    return parts


def _split3(x):
    return _split_bf16(x, 3)


def _split2(x):
    return _split_bf16(x, 2)


def _dot_split(parts, mask_bf, *, lhs_is_mask):
    acc = None
    for p in reversed(parts):
        t = (jnp.dot(mask_bf, p, preferred_element_type=F32) if lhs_is_mask
             else jnp.dot(p, mask_bf, preferred_element_type=F32))
        acc = t if acc is None else acc + t
    return acc


def _adaln_kernel(cv_ref, w_ref, b_ref, o_ref):
    s = _silu(cv_ref[...]).astype(BF16)
    w = w_ref[...].astype(BF16)
    o_ref[...] = jnp.dot(s, w, preferred_element_type=F32) + b_ref[...]


def _adaln(cvec, mod_w, mod_b, bn=1024):
    depth, d, n = mod_w.shape
    return pl.pallas_call(
        _adaln_kernel,
        grid=(depth, n // bn),
        in_specs=[
            pl.BlockSpec((N_SEG_PAD, d), lambda l, j: (0, 0)),
            pl.BlockSpec((None, d, bn), lambda l, j: (l, 0, j)),
            pl.BlockSpec((None, 1, bn), lambda l, j: (l, 0, j)),
        ],
        out_specs=pl.BlockSpec((None, N_SEG_PAD, bn), lambda l, j: (l, 0, j)),
        out_shape=jax.ShapeDtypeStruct((depth, N_SEG_PAD, n), F32),
        compiler_params=_params("arbitrary", "arbitrary"),
        name="adaln",
    )(cvec, mod_w, mod_b.reshape(depth, 1, n))


class _Rows:
    def __init__(self, n_ctx_rows, dec_seq, d_model):
        self.n_ctx = n_ctx_rows
        self.dec_seq = dec_seq
        self.d = d_model

    def seg(self, i, bm):
        r = i * bm
        return jnp.where(r < self.n_ctx, 0, 1 + (r - self.n_ctx) // self.dec_seq)


def _mod_spec(rows, layer, k, bm):
    def idx(i, *_):
        return ((layer * N_SEG_PAD + rows.seg(i, bm)) * N_MOD + k, 0, 0)

    return pl.BlockSpec((None, 1, rows.d), idx)


def _modulate_kernel(x_ref, sh_ref, sc_ref, h_ref):
    h_ref[...] = (x_ref[...] * (1.0 + sc_ref[...]) + sh_ref[...]).astype(BF16)


def _modulate(x, modv, rows, layer, k_shift, k_scale, bm=512):
    m, d = x.shape
    return pl.pallas_call(
        _modulate_kernel,
        grid=(m // bm,),
        in_specs=[
            pl.BlockSpec((bm, d), lambda i: (i, 0)),
            _mod_spec(rows, layer, k_shift, bm),
            _mod_spec(rows, layer, k_scale, bm),
        ],
        out_specs=pl.BlockSpec((bm, d), lambda i: (i, 0)),
        out_shape=jax.ShapeDtypeStruct((m, d), BF16),
        compiler_params=_params("parallel"),
        name="modulate",
    )(x, modv, modv)


def _ffn_in_kernel(a_ref, wg_ref, wu_ref, o_ref, w_scr):
    bn = wg_ref.shape[1]

    @pl.when(pl.program_id(1) == 0)
    def _():
        w_scr[:, :bn] = wg_ref[...].astype(BF16)
        w_scr[:, bn:] = wu_ref[...].astype(BF16)

    u = jnp.dot(a_ref[...], w_scr[...], preferred_element_type=F32)
    o_ref[...] = (_silu(u[:, :bn]) * u[:, bn:]).astype(BF16)


def _ffn_in(a, w_in, lead, bm=1024, bn=512):
    m, k = a.shape
    f = w_in.shape[-1] // 2
    nt = f // bn
    return pl.pallas_call(
        _ffn_in_kernel,
        grid=(nt, m // bm),
        in_specs=[
            pl.BlockSpec((bm, k), lambda j, i: (i, 0)),
            _lead_spec(lead, (k, bn), lambda j, i: (0, j)),
            _lead_spec(lead, (k, bn), lambda j, i: (0, j + nt)),
        ],
        out_specs=pl.BlockSpec((bm, bn), lambda j, i: (i, j)),
        out_shape=jax.ShapeDtypeStruct((m, f), BF16),
        scratch_shapes=[pltpu.VMEM((k, 2 * bn), BF16)],
        compiler_params=_params("arbitrary", "arbitrary"),
        name="ffn_in",
    )(a, w_in, w_in)


def _proj_kernel(a_ref, w_ref, o_ref, w_scr, *, scale_first, w_t):
    @pl.when(pl.program_id(1) == 0)
    def _():
        w = w_ref[...]
        w_scr[...] = (jnp.transpose(w) if w_t else w).astype(BF16)

    u = jnp.dot(a_ref[...], w_scr[...], preferred_element_type=F32)
    if scale_first is not None:
        u = u * jnp.where(pl.program_id(0) == 0, scale_first, 1.0)
    o_ref[...] = u.astype(o_ref.dtype)


def _proj(a, w, lead, col0, ncols, out_dtype, bm=1024, bn=1024, scale_first=None, w_t=False, name="proj"):
    m, k = a.shape
    assert col0 % bn == 0 and ncols % bn == 0
    j0 = col0 // bn
    w_spec = (_lead_spec(lead, (bn, k), lambda j, i: (j + j0, 0)) if w_t
              else _lead_spec(lead, (k, bn), lambda j, i: (0, j + j0)))
    return pl.pallas_call(
        functools.partial(_proj_kernel, scale_first=scale_first, w_t=w_t),
        grid=(ncols // bn, m // bm),
        in_specs=[
            pl.BlockSpec((bm, k), lambda j, i: (i, 0)),
            w_spec,
        ],
        out_specs=pl.BlockSpec((bm, bn), lambda j, i: (i, j)),
        out_shape=jax.ShapeDtypeStruct((m, ncols), out_dtype),
        scratch_shapes=[pltpu.VMEM((k, bn), BF16)],
        compiler_params=_params("arbitrary", "arbitrary"),
        name=name,
    )(a, w)


def _small_proj_kernel(a_ref, w_ref, o_ref, *, w_t):
    w = w_ref[...].astype(BF16)
    if w_t:
        o_ref[...] = lax.dot_general(a_ref[...], w, (((1,), (1,)), ((), ())), preferred_element_type=F32)
    else:
        o_ref[...] = jnp.dot(a_ref[...], w, preferred_element_type=F32)


def _small_proj(a, w, lead, col_block, ncols, bm=1024, w_t=False, name="small_proj"):
    m, k = a.shape
    w_spec = (_lead_spec(lead, (ncols, k), lambda i: (col_block, 0)) if w_t
              else _lead_spec(lead, (k, ncols), lambda i: (0, col_block)))
    return pl.pallas_call(
        functools.partial(_small_proj_kernel, w_t=w_t),
        grid=(m // bm,),
        in_specs=[
            pl.BlockSpec((bm, k), lambda i: (i, 0)),
            w_spec,
        ],
        out_specs=pl.BlockSpec((bm, ncols), lambda i: (i, 0)),
        out_shape=jax.ShapeDtypeStruct((m, ncols), F32),
        compiler_params=_params("parallel"),
        name=name,
    )(a, w)


def _out_ln_kernel(*refs, alpha, coef, has_next, has_ssq, ssq_dim):
    it = iter(refs)
    a_ref, w_ref, x_hbm, gate_ref, g_ref, b_ref = (next(it) for _ in range(6))
    sh_ref = sc_ref = ssq_ref = None
    if has_next:
        sh_ref, sc_ref = next(it), next(it)
    if has_ssq:
        ssq_ref = next(it)
    xo_ref = next(it)
    ho_ref = next(it) if has_next else None
    x_ref, x_sem = next(it), next(it)

    k = pl.program_id(1)
    bm = xo_ref.shape[0]

    def x_copy():
        r0 = pl.multiple_of(pl.program_id(0) * bm, bm)
        return pltpu.make_async_copy(x_hbm.at[pl.ds(r0, bm), :], x_ref, x_sem)

    @pl.when(k == 0)
    def _():
        x_copy().start()
        xo_ref[...] = jnp.dot(a_ref[...], w_ref[...].astype(BF16), preferred_element_type=F32)

    @pl.when(k != 0)
    def _():
        xo_ref[...] += jnp.dot(a_ref[...], w_ref[...].astype(BF16), preferred_element_type=F32)

    @pl.when(k == pl.num_programs(1) - 1)
    def _():
        x_copy().wait()
        d_model = xo_ref.shape[1]
        tiles = [slice(t * LANES, (t + 1) * LANES) for t in range(d_model // LANES)]

        def slab(si, carry):
            rs = pl.ds(pl.multiple_of(si * LN_ROWS, LN_ROWS), LN_ROWS)
            r = lax.rsqrt(ssq_ref[rs, :] * (1.0 / ssq_dim) + LN_EPS) if has_ssq else None
            acc = None
            for cs in tiles:
                y = xo_ref[rs, cs]
                if has_ssq:
                    y = y * r
                z = alpha * x_ref[rs, cs] + (coef * gate_ref[:, cs]) * y
                xo_ref[rs, cs] = z
                acc = z if acc is None else acc + z
            mu = jnp.sum(acc, axis=1, keepdims=True) * (1.0 / d_model)
            acc = None
            for cs in tiles:
                zc = xo_ref[rs, cs] - mu
                xo_ref[rs, cs] = zc
                acc = zc * zc if acc is None else acc + zc * zc
            rstd = lax.rsqrt(jnp.sum(acc, axis=1, keepdims=True) * (1.0 / d_model) + LN_EPS)
            for cs in tiles:
                xn = xo_ref[rs, cs] * rstd * g_ref[:, cs] + b_ref[:, cs]
                xo_ref[rs, cs] = xn
                if has_next:
                    ho_ref[rs, cs] = (xn * (1.0 + sc_ref[:, cs]) + sh_ref[:, cs]).astype(BF16)
            return carry

        lax.fori_loop(0, xo_ref.shape[0] // LN_ROWS, slab, 0, unroll=LN_UNROLL)


def _out_ln(a, w, lead, x, modv, lnv, rows, *, layer, k_gate, ln_idx, alpha, coef,
            nxt=None, ssq=None, ssq_dim=None, bm=1024, bk=512, name="out_ln"):
    m, kdim = a.shape
    d = x.shape[1]
    assert kdim % bk == 0 and kdim // bk >= 2
    in_specs = [
        pl.BlockSpec((bm, bk), lambda i, k: (i, k)),
        _lead_spec(lead, (bk, d), lambda i, k: (k, 0)),
        pl.BlockSpec(memory_space=pl.ANY),
        _mod_spec(rows, layer, k_gate, bm),
        pl.BlockSpec((None, 1, d), lambda i, k: (2 * ln_idx, 0, 0)),
        pl.BlockSpec((None, 1, d), lambda i, k: (2 * ln_idx + 1, 0, 0)),
    ]
    args = [a, w, x, modv, lnv, lnv]
    if nxt is not None:
        in_specs += [_mod_spec(rows, nxt[0], nxt[1], bm), _mod_spec(rows, nxt[0], nxt[2], bm)]
        args += [modv, modv]
    if ssq is not None:
        in_specs.append(pl.BlockSpec((bm, ssq.shape[1]), lambda i, k: (i, 0)))
        args.append(ssq)
    out_specs = [pl.BlockSpec((bm, d), lambda i, k: (i, 0))]
    out_shape = [jax.ShapeDtypeStruct((m, d), F32)]
    if nxt is not None:
        out_specs.append(pl.BlockSpec((bm, d), lambda i, k: (i, 0)))
        out_shape.append(jax.ShapeDtypeStruct((m, d), BF16))
    res = pl.pallas_call(
        functools.partial(_out_ln_kernel, alpha=alpha, coef=coef, has_next=nxt is not None,
                          has_ssq=ssq is not None, ssq_dim=ssq_dim),
        grid=(m // bm, kdim // bk),
        in_specs=in_specs,
        out_specs=out_specs,
        out_shape=out_shape,
        scratch_shapes=[pltpu.VMEM((bm, d), F32), pltpu.SemaphoreType.DMA(())],
        compiler_params=_params("arbitrary", "arbitrary"),
        name=name,
    )(*args)
    return (res[0], res[1]) if nxt is not None else (res[0], None)


def _mlstm_kernel(*refs, n_chunks, hp, has_init, has_state_out):
    it = iter(refs)
    q_ref, k_ref, v_ref, o_ref, gr_ref, br_ref, ng_ref = (next(it) for _ in range(7))
    c0_ref = n0_ref = m0_ref = None
    if has_init:
        c0_ref, n0_ref, m0_ref = next(it), next(it), next(it)
    y_ref = next(it)
    co_ref = no_ref = mo_ref = None
    if has_state_out:
        co_ref, no_ref, mo_ref = next(it), next(it), next(it)
    hf_scr, hb_scr, kt_scr, vx_scr, cn_scr, m_scr = (next(it) for _ in range(6))

    L, dk, dv = ML_CHUNK, ML_DK, ML_DV
    t_idx = lax.broadcasted_iota(jnp.int32, (L, L), 0)
    s_idx = lax.broadcasted_iota(jnp.int32, (L, L), 1)
    ones_bf = jnp.ones((L, LANES), BF16)
    masks = []
    for d in range(2):
        incl = (s_idx <= t_idx) if d == 0 else (s_idx >= t_idx)
        incl_t = (t_idx <= s_idx) if d == 0 else (t_idx >= s_idx)
        masks.append((incl, jnp.where(incl_t, 1.0, 0.0).astype(BF16)))

    for hh in range(hp):
        vx_scr[:, hh * (dv + LANES):hh * (dv + LANES) + dv] = v_ref[:, hh * dv:(hh + 1) * dv]
        vx_scr[:, hh * (dv + LANES) + dv:(hh + 1) * (dv + LANES)] = jnp.ones((v_ref.shape[0], LANES), BF16)
        for c in range(n_chunks):
            kt_scr[c, hh * dk:(hh + 1) * dk, :] = jnp.transpose(
                k_ref[c * L:(c + 1) * L, hh * dk:(hh + 1) * dk].astype(F32))
        for d in range(2):
            i = hh * 2 + d
            if has_init:
                cn_scr[i, :, :dv] = c0_ref[d, hh]
                cn_scr[i, :, dv:] = jnp.transpose(jnp.broadcast_to(n0_ref[d, hh], (LANES, dk)))
                m_scr[i] = m0_ref[d, hh]
            else:
                cn_scr[i] = jnp.zeros((dk, dv + LANES), F32)
                m_scr[i] = jnp.zeros((1, 1), F32)

    def step(ci, carry):
        chains = []
        for d in range(2):
            incl, incl_t_bf = masks[d]
            c = ci if d == 0 else n_chunks - 1 - ci
            r0 = pl.multiple_of(c * L, L)
            last = L - 1 if d == 0 else 0
            grow = gr_ref[c] + br_ref[...]
            lf_parts = _split3(_log_sigmoid(grow))
            cum_r = _dot_split(lf_parts, incl_t_bf, lhs_is_mask=False)
            lf_parts = [p.astype(F32) for p in lf_parts]
            for hh in range(hp):
                gi, gf = hh * 4 + d, hh * 4 + 2 + d
                ch = dict(d=d, hh=hh, i=hh * 2 + d, c=c, r0=r0, incl=incl,
                          li_r=grow[gi:gi + 1, :], b_r=cum_r[gf:gf + 1, :],
                          lf_rows=[p[gf:gf + 1, :] for p in lf_parts])
                ch["g"] = ch["b_r"][:, last:last + 1]
                chains.append(ch)

        for ch in chains:
            b_c = None
            for p in reversed(ch["lf_rows"]):
                t = jnp.dot(jnp.where(ch["incl"], p, 0.0).astype(BF16), ones_bf, preferred_element_type=F32)
                b_c = t if b_c is None else b_c + t
            ch["b_c"] = b_c
        for ch in chains:
            hh, r0 = ch["hh"], ch["r0"]
            qc = q_ref[pl.ds(r0, L), hh * dk:(hh + 1) * dk]
            kc = k_ref[pl.ds(r0, L), hh * dk:(hh + 1) * dk]
            ch["cn_prev"] = cn_scr[ch["i"]]
            ch["m_prev"] = m_scr[ch["i"]]
            ch["qk"] = lax.dot_general(qc, kc, (((1,), (1,)), ((), ())), preferred_element_type=F32)
            ch["q_cn"] = jnp.dot(qc, ch["cn_prev"].astype(BF16), preferred_element_type=F32)
        for ch in chains:
            hh, r0 = ch["hh"], ch["r0"]
            dmat = jnp.where(ch["incl"], ch["b_c"] - ch["b_r"] + ch["li_r"], -jnp.inf)
            inter = ch["b_c"] + ch["m_prev"]
            m_t = jnp.maximum(inter, jnp.max(dmat, axis=1, keepdims=True))
            s_bf = (ch["qk"] * jnp.exp(dmat - m_t)).astype(BF16)
            ch["sc"] = jnp.exp(inter - m_t)
            ch["floor"] = jnp.exp(-m_t)
            ch["vc"] = vx_scr[pl.ds(r0, L), hh * (dv + LANES):(hh + 1) * (dv + LANES)]
            ch["s_v"] = jnp.dot(s_bf, ch["vc"], preferred_element_type=F32)
        for ch in chains:
            hh, r0, sc, q_cn, s_v = ch["hh"], ch["r0"], ch["sc"], ch["q_cn"], ch["s_v"]
            hs_scr = hf_scr if ch["d"] == 0 else hb_scr
            den = sc * q_cn[:, dv:] + s_v[:, dv:]
            inv = 1.0 / jnp.maximum(jnp.abs(den), ch["floor"])
            for cb in range(dv // LANES):
                cs = slice(cb * LANES, (cb + 1) * LANES)
                hs_scr[pl.ds(r0, L), hh * dv + cb * LANES:hh * dv + (cb + 1) * LANES] = (
                    sc * q_cn[:, cs] + s_v[:, cs]) * inv
        for ch in chains:
            hh, g, m_prev = ch["hh"], ch["g"], ch["m_prev"]
            ds_r = g - ch["b_r"] + ch["li_r"]
            m_new = jnp.maximum(g + m_prev, jnp.max(ds_r, axis=1, keepdims=True))
            ktw = (kt_scr[ch["c"], hh * dk:(hh + 1) * dk, :] * jnp.exp(ds_r - m_new)).astype(BF16)
            decay = jnp.exp(g + m_prev - m_new)
            cn_scr[ch["i"]] = decay * ch["cn_prev"] + jnp.dot(ktw, ch["vc"], preferred_element_type=F32)
            m_scr[ch["i"]] = m_new
        return carry

    lax.fori_loop(0, n_chunks, step, 0)

    for hh in range(hp):
        if has_state_out:
            for d in range(2):
                co_ref[d, hh] = cn_scr[hh * 2 + d, :, :dv]
                no_ref[d, hh] = jnp.transpose(cn_scr[hh * 2 + d, :, dv:])[0:1, :]
                mo_ref[d, hh] = m_scr[hh * 2 + d]
        cols = slice(hh * dv, (hh + 1) * dv)
        hs = hf_scr[:, cols] + hb_scr[:, cols]
        mu = jnp.mean(hs, axis=1, keepdims=True)
        hc = hs - mu
        var = jnp.mean(hc * hc, axis=1, keepdims=True)
        hn = hc * lax.rsqrt(var + LN_EPS) * ng_ref[:, cols]
        y_ref[:, cols] = (jax.nn.sigmoid(o_ref[:, cols].astype(F32)) * hn).astype(BF16)


def _mlstm_core(qkv, o, gates, b_gate, norm_g, *, batch, seq, row_blk0, init=None, state_out=None, y_prev=None):
    m = qkv.shape[0]
    h, dk, dv, L, hp = ML_HEADS, ML_DK, ML_DV, ML_CHUNK, ML_HEADS_PER_STEP
    hg = h // hp
    nc = seq // L
    r0 = row_blk0 * seq
    g4 = jnp.swapaxes(gates[r0:r0 + batch * seq].reshape(batch, seq, 4, h), 2, 3)
    grow = jnp.transpose(g4.reshape(batch, nc, L, hg, hp * 4), (0, 3, 1, 4, 2))
    brow = jnp.transpose(b_gate.reshape(4, h)).reshape(hg, hp * 4, 1)
    kq = (h * dk) // (hp * dk)
    kv = (2 * h * dk) // (hp * dv)
    in_specs = [
        pl.BlockSpec((seq, hp * dk), lambda b, g: (row_blk0 + b, g)),
        pl.BlockSpec((seq, hp * dk), lambda b, g: (row_blk0 + b, kq + g)),
        pl.BlockSpec((seq, hp * dv), lambda b, g: (row_blk0 + b, kv + g)),
        pl.BlockSpec((seq, hp * dv), lambda b, g: (row_blk0 + b, g)),
        pl.BlockSpec((None, None, nc, hp * 4, L), lambda b, g: (b, g, 0, 0, 0)),
        pl.BlockSpec((None, hp * 4, 1), lambda b, g: (g, 0, 0)),
        pl.BlockSpec((1, hp * dv), lambda b, g: (0, g)),
    ]
    args = [qkv, qkv, qkv, o, grow, brow, norm_g.reshape(1, h * dv)]
    if init is not None:
        c0, n0, m0, lyr = init
        nl = c0.shape[1]
        in_specs += [
            pl.BlockSpec((None, None, 2, hp, dk, dv), lambda b, g: (b, lyr, 0, g, 0, 0)),
            pl.BlockSpec((None, None, 2, hp, 1, dk), lambda b, g: (b, lyr, 0, g, 0, 0)),
            pl.BlockSpec((None, None, 2, hp, 1, 1), lambda b, g: (b, lyr, 0, g, 0, 0)),
        ]
        args += [c0, n0.reshape(batch, nl, 2, h, 1, dk), m0.reshape(batch, nl, 2, h, 1, 1)]
    out_specs = [pl.BlockSpec((seq, hp * dv), lambda b, g: (row_blk0 + b, g))]
    out_shape = [jax.ShapeDtypeStruct((m, h * dv), BF16)]
    prev = {} if y_prev is None else {0: y_prev}
    if state_out is not None:
        nl_out, slot, st_prev = state_out
        out_specs += [
            pl.BlockSpec((None, None, 2, hp, dk, dv), lambda b, g: (b, slot, 0, g, 0, 0)),
            pl.BlockSpec((None, None, 2, hp, 1, dk), lambda b, g: (b, slot, 0, g, 0, 0)),
            pl.BlockSpec((None, None, 2, hp, 1, 1), lambda b, g: (b, slot, 0, g, 0, 0)),
        ]
        out_shape += [
            jax.ShapeDtypeStruct((batch, nl_out, 2, h, dk, dv), F32),
            jax.ShapeDtypeStruct((batch, nl_out, 2, h, 1, dk), F32),
            jax.ShapeDtypeStruct((batch, nl_out, 2, h, 1, 1), F32),
        ]
        if st_prev is not None:
            prev.update({1: st_prev[0], 2: st_prev[1], 3: st_prev[2]})
    return _call_aliased(
        functools.partial(_mlstm_kernel, n_chunks=nc, hp=hp, has_init=init is not None,
                          has_state_out=state_out is not None),
        prev,
        grid=(batch, hg),
        in_specs=in_specs,
        args=args,
        out_specs=out_specs,
        out_shape=out_shape,
        scratch_shapes=[
            pltpu.VMEM((seq, hp * dv), F32),
            pltpu.VMEM((seq, hp * dv), F32),
            pltpu.VMEM((nc, hp * dk, L), F32),
            pltpu.VMEM((seq, hp * (dv + LANES)), BF16),
            pltpu.VMEM((2 * hp, dk, dv + LANES), F32),
            pltpu.VMEM((2 * hp, 1, 1), F32),
        ],
        compiler_params=_params("parallel", "parallel"),
        name="mlstm_core",
    )


def _rope(x, cos_t, sin_t):
    lane = lax.broadcasted_iota(jnp.int32, x.shape, 1)
    first = (lane % (2 * ROPE_PAIRS)) < ROPE_PAIRS
    rot = jnp.where(first, pltpu.roll(x, x.shape[1] - ROPE_PAIRS, 1), pltpu.roll(x, ROPE_PAIRS, 1))
    return x * cos_t + rot * sin_t


def _rms(x, g):
    return x * lax.rsqrt(jnp.mean(x * x, axis=1, keepdims=True) + LN_EPS) * g


def _attn_kernel(*refs, seq, has_ctx):
    it = iter(refs)
    q_ref, k_ref, v_ref, qg_ref, kg_ref = (next(it) for _ in range(5))
    cos_ref = sin_ref = ck_ref = cv_ref = None
    if has_ctx:
        cos_ref, sin_ref, ck_ref, cv_ref = (next(it) for _ in range(4))
    o_ref = next(it)
    ko_ref = vo_ref = None
    if not has_ctx:
        ko_ref, vo_ref = next(it), next(it)
    keys_scr, vals_scr = next(it), next(it)

    hd = ATTN_HD
    kn = _rms(k_ref[...].astype(F32), kg_ref[...])
    if has_ctx:
        keys_scr[:seq, :] = _rope(kn, cos_ref[...], sin_ref[...]).astype(BF16)
        keys_scr[seq:, :] = ck_ref[...].astype(BF16)
        vals_scr[:seq, :] = v_ref[...]
        vals_scr[seq:, :] = cv_ref[...].astype(BF16)
    else:
        ko_ref[...] = kn
        vo_ref[...] = v_ref[...].astype(F32)
        keys_scr[...] = kn.astype(BF16)
        vals_scr[...] = v_ref[...]

    qb = min(ATTN_QBLOCK, seq)
    scale = hd ** -0.5

    def qblock(bi, carry):
        r0 = pl.multiple_of(bi * qb, qb)
        ss = []
        for g in range(ATTN_GROUP):
            q = _rms(q_ref[pl.ds(r0, qb), g * hd:(g + 1) * hd].astype(F32), qg_ref[...])
            if has_ctx:
                q = _rope(q, cos_ref[pl.ds(r0, qb), :], sin_ref[pl.ds(r0, qb), :])
            q = (q * scale).astype(BF16)
            ss.append(lax.dot_general(q, keys_scr[...], (((1,), (1,)), ((), ())), preferred_element_type=F32))
        ps, dens = [], []
        for s in ss:
            p = jnp.exp(s - jnp.max(s, axis=1, keepdims=True))
            dens.append(jnp.sum(p, axis=1, keepdims=True))
            ps.append(p.astype(BF16))
        for g in range(ATTN_GROUP):
            o = jnp.dot(ps[g], vals_scr[...], preferred_element_type=F32) / dens[g]
            o_ref[pl.ds(r0, qb), g * hd:(g + 1) * hd] = o.astype(BF16)
        return carry

    lax.fori_loop(0, seq // qb, qblock, 0)


def _attn_core(qkv, q_g, k_g, *, batch, seq, row_blk0, rope=None, ctx=None, y_prev=None):
    m = qkv.shape[0]
    hd, kvh, grp = ATTN_HD, ATTN_KV_HEADS, ATTN_GROUP
    has_ctx = ctx is not None
    k_off = ATTN_HEADS
    v_off = ATTN_HEADS + kvh
    in_specs = [
        pl.BlockSpec((seq, grp * hd), lambda b, kv: (row_blk0 + b, kv)),
        pl.BlockSpec((seq, hd), lambda b, kv: (row_blk0 + b, k_off + kv)),
        pl.BlockSpec((seq, hd), lambda b, kv: (row_blk0 + b, v_off + kv)),
        pl.BlockSpec((1, hd), lambda b, kv: (0, 0)),
        pl.BlockSpec((1, hd), lambda b, kv: (0, 0)),
    ]
    args = [qkv, qkv, qkv, q_g.reshape(1, hd), k_g.reshape(1, hd)]
    n_keys = seq
    if has_ctx:
        cos_t, sin_t = rope
        ck, cv = ctx
        past = ck.shape[1]
        n_keys = seq + past
        in_specs += [
            pl.BlockSpec((seq, hd), lambda b, kv: (0, 0)),
            pl.BlockSpec((seq, hd), lambda b, kv: (0, 0)),
            pl.BlockSpec((None, past, hd), lambda b, kv: (b, 0, kv)),
            pl.BlockSpec((None, past, hd), lambda b, kv: (b, 0, kv)),
        ]
        args += [cos_t, sin_t, ck, cv]
    out_specs = [pl.BlockSpec((seq, grp * hd), lambda b, kv: (row_blk0 + b, kv))]
    out_shape = [jax.ShapeDtypeStruct((m, ATTN_HEADS * hd), BF16)]
    if not has_ctx:
        out_specs += [pl.BlockSpec((None, seq, hd), lambda b, kv: (b, 0, kv))] * 2
        out_shape += [jax.ShapeDtypeStruct((batch, seq, kvh * hd), F32)] * 2
    return _call_aliased(
        functools.partial(_attn_kernel, seq=seq, has_ctx=has_ctx),
        {} if y_prev is None else {0: y_prev},
        grid=(batch, kvh),
        in_specs=in_specs,
        args=args,
        out_specs=out_specs,
        out_shape=out_shape,
        scratch_shapes=[pltpu.VMEM((n_keys, hd), BF16), pltpu.VMEM((n_keys, hd), BF16)],
        compiler_params=_params("parallel", "parallel"),
        name="attn_core",
    )


def _rope_tables(seq):
    rows = seq // GRID_W
    row = jnp.repeat(jnp.arange(rows, dtype=F32), GRID_W)
    col = jnp.tile(jnp.arange(GRID_W, dtype=F32), rows)
    freqs = ROPE_THETA ** (-jnp.arange(ROPE_PAIRS, dtype=F32) / ROPE_PAIRS)
    ar, ac = row[:, None] * freqs, col[:, None] * freqs
    cos_t = jnp.concatenate([jnp.cos(ar), jnp.cos(ar), jnp.cos(ac), jnp.cos(ac)], axis=1)
    sin_t = jnp.concatenate([-jnp.sin(ar), jnp.sin(ar), -jnp.sin(ac), jnp.sin(ac)], axis=1)
    return cos_t, sin_t


def _dwconv_silu_chunk(xpad_ref, r0, w, b):
    L, padr = SSD_CHUNK, CONV_PAD_ROWS
    win = xpad_ref[pl.ds(r0, L + 2 * padr), :]
    t_idx = lax.broadcasted_iota(jnp.int32, (L, L + 2 * padr), 0)
    s_idx = lax.broadcasted_iota(jnp.int32, (L, L + 2 * padr), 1)
    acc = b
    for j in range(SSD_CONV_W):
        off = j - SSD_CONV_W // 2
        if off == 0:
            xs = win[padr:padr + L, :].astype(F32)
        else:
            shift = jnp.where(s_idx == t_idx + (padr + off), 1.0, 0.0).astype(BF16)
            xs = jnp.dot(shift, win, preferred_element_type=F32)
        acc = acc + xs * w[j:j + 1, :]
    return _silu(acc)


def _ssd_kernel(*refs, n_chunks, has_init, has_state_out):
    it = iter(refs)
    (z_ref, x_ref, b_ref, c_ref, wx_ref, wb_ref, wc_ref, bx_ref, bb_ref, bc_ref,
     dtc_ref, dtr_ref, dbc_ref, dbr_ref, alc_ref, alr_ref, dsk_ref, ng_ref) = (next(it) for _ in range(18))
    h0_ref = next(it) if has_init else None
    y_ref, ssq_ref = next(it), next(it)
    ho_ref = next(it) if has_state_out else None
    (xpad_scr, xs_scr, xsb_scr, xst_scr, bm_scr, cm_scr, dtc_scr,
     yf_scr, yb_scr, h_scr) = (next(it) for _ in range(10))

    L, P, N = SSD_CHUNK, SSD_HD, SSD_N
    gw = x_ref.shape[1]
    E = gw // P
    n_pairs = gw // LANES
    seq, padr = x_ref.shape[0], CONV_PAD_ROWS

    cw = gw + 2 * N
    xpad_scr[0:padr, :] = jnp.zeros((padr, cw), BF16)
    xpad_scr[padr + seq:, :] = jnp.zeros((padr, cw), BF16)
    xpad_scr[padr:padr + seq, 0:gw] = x_ref[...]
    xpad_scr[padr:padr + seq, gw:gw + N] = b_ref[...]
    xpad_scr[padr:padr + seq, gw + N:] = c_ref[...]
    conv_w = jnp.concatenate([wx_ref[...], wb_ref[...], wc_ref[...]], axis=1)
    conv_b = jnp.concatenate([bx_ref[...], bb_ref[...], bc_ref[...]], axis=1)
    for c in range(n_chunks):
        rows = slice(c * L, (c + 1) * L)
        xbc = _dwconv_silu_chunk(xpad_scr, c * L, conv_w, conv_b)
        xs_scr[rows, :] = xbc[:, :gw]
        xsb_scr[rows, :] = xbc[:, :gw].astype(BF16)
        bm_scr[rows, :] = xbc[:, gw:gw + N].astype(BF16)
        cm_scr[rows, :] = xbc[:, gw + N:].astype(BF16)
        for j in range(n_pairs):
            xst_scr[c, j * LANES:(j + 1) * LANES, :] = jnp.transpose(xbc[:, j * LANES:(j + 1) * LANES])
    dtc_scr[...] = _softplus(dtc_ref[...] + dbc_ref[...])
    a_c = -jnp.exp(alc_ref[...])
    a_r = -jnp.exp(alr_ref[...])

    t_idx = lax.broadcasted_iota(jnp.int32, (L, L), 0)
    s_idx = lax.broadcasted_iota(jnp.int32, (L, L), 1)
    lane_lo = lax.broadcasted_iota(jnp.int32, (L, LANES), 1) < P
    ex_row = lax.broadcasted_iota(jnp.int32, (2 * E, gw), 0)
    ex_head = jnp.right_shift(lax.broadcasted_iota(jnp.int32, (2 * E, gw), 1), P.bit_length() - 1)

    consts = []
    for d in range(2):
        incl = (s_idx <= t_idx) if d == 0 else (s_idx >= t_idx)
        incl_bf = jnp.where(incl, 1.0, 0.0).astype(BF16)
        incl_t_bf = jnp.where((t_idx <= s_idx) if d == 0 else (t_idx >= s_idx), 1.0, 0.0).astype(BF16)
        expand_bf = jnp.where(ex_row == ex_head + d * E, 1.0, 0.0).astype(BF16)
        consts.append((incl, incl_bf, incl_t_bf, expand_bf))
        if has_init:
            for e in range(E):
                h_scr[d, e * P:(e + 1) * P, :] = h0_ref[d, e]
        else:
            h_scr[d] = jnp.zeros((E * P, N), F32)

    def chunk(ci, carry):
        sd = []
        for d in range(2):
            incl, incl_bf, incl_t_bf, expand_bf = consts[d]
            last = L - 1 if d == 0 else 0
            c = ci if d == 0 else n_chunks - 1 - ci
            r0 = pl.multiple_of(c * L, L)
            s = dict(d=d, c=c, r0=r0, incl=incl, bc=bm_scr[pl.ds(r0, L), :], cc=cm_scr[pl.ds(r0, L), :],
                     h_all=h_scr[d])
            dt_r = _softplus(dtr_ref[c] + dbr_ref[...])
            da_c = dtc_scr[pl.ds(r0, L), :] * a_c
            da_r = dt_r * a_r
            s["cs_c"] = _dot_split(_split3(da_c), incl_bf, lhs_is_mask=True)
            cs_r = _dot_split(_split3(da_r), incl_t_bf, lhs_is_mask=False)
            s["cs_rd"] = cs_r[d * E:(d + 1) * E, :]
            s["dt_rd"] = dt_r[d * E:(d + 1) * E, :]
            s["tot_r"] = s["cs_rd"][:, last:last + 1]
            s["cb"] = lax.dot_general(s["cc"], s["bc"], (((1,), (1,)), ((), ())), preferred_element_type=F32)
            s["y_off"] = lax.dot_general(s["cc"], s["h_all"].astype(BF16), (((1,), (1,)), ((), ())),
                                         preferred_element_type=F32)
            s["expand_bf"] = expand_bf
            sd.append(s)
        for s in sd:
            s["ecs"] = _dot_split(_split2(jnp.exp(s["cs_c"])), s["expand_bf"], lhs_is_mask=False)
        for s in sd:
            d = s["d"]
            s["m"] = []
            for e in range(E):
                seg = s["cs_c"][:, d * E + e:d * E + e + 1] - s["cs_rd"][e:e + 1, :]
                m_e = s["cb"] * jnp.exp(jnp.where(s["incl"], seg, -jnp.inf)) * s["dt_rd"][e:e + 1, :]
                s["m"].append(m_e.astype(BF16))
        for s in sd:
            ys = []
            for j in range(n_pairs):
                xp = xsb_scr[pl.ds(s["r0"], L), j * LANES:(j + 1) * LANES]
                outs = [jnp.dot(s["m"][j * (LANES // P) + q], xp, preferred_element_type=F32)
                        for q in range(LANES // P)]
                ys.append(jnp.where(lane_lo, outs[0], outs[1]))
            y_scr = yf_scr if s["d"] == 0 else yb_scr
            y_scr[pl.ds(s["r0"], L), :] = jnp.concatenate(ys, axis=1) + s["ecs"] * s["y_off"]
        for s in sd:
            d, tot_r, h_all = s["d"], s["tot_r"], s["h_all"]
            wr = jnp.exp(tot_r - s["cs_rd"]) * s["dt_rd"]
            wr_big = jnp.concatenate([jnp.broadcast_to(wr[e:e + 1, :], (P, L)) for e in range(E)], axis=0)
            upd = jnp.dot((xst_scr[s["c"]] * wr_big).astype(BF16), s["bc"], preferred_element_type=F32)
            etot = jnp.exp(tot_r)
            for e in range(E):
                h_scr[d, e * P:(e + 1) * P, :] = (etot[e:e + 1, :] * h_all[e * P:(e + 1) * P, :]
                                                  + upd[e * P:(e + 1) * P, :])
        return carry

    lax.fori_loop(0, n_chunks, chunk, 0)
    if has_state_out:
        for d in range(2):
            for e in range(E):
                ho_ref[d, e] = h_scr[d, e * P:(e + 1) * P, :]

    y = yf_scr[...] + yb_scr[...] + dsk_ref[...] * xs_scr[...]
    yz = y * _silu(z_ref[...].astype(F32))
    ssq = jnp.sum(yz * yz, axis=1, keepdims=True)

    @pl.when(pl.program_id(1) == 0)
    def _():
        ssq_ref[...] = ssq

    @pl.when(pl.program_id(1) != 0)
    def _():
        ssq_ref[...] += ssq

    y_ref[...] = (yz * ng_ref[...]).astype(BF16)


def _ssd_core(zxbc, dt_raw, conv_w, conv_b, dt_bias, a_log, d_skip, norm_g, *,
              batch, seq, row_blk0, init=None, state_out=False, prev=None):
    m = zxbc.shape[0]
    G, P, N, L = SSD_GROUPS, SSD_HD, SSD_N, SSD_CHUNK
    di = d_skip.shape[0] * P
    heads = di // P
    E = heads // G
    gw = E * P
    nc = seq // L
    r0 = row_blk0 * seq
    dt4 = dt_raw[r0:r0 + batch * seq].reshape(batch, seq, 2, G, E)
    dtc = jnp.transpose(dt4, (0, 3, 1, 2, 4)).reshape(batch, G, seq, 2 * E)
    dtr = jnp.transpose(dt4.reshape(batch, nc, L, 2, G, E), (0, 4, 1, 3, 5, 2)).reshape(batch, G, nc, 2 * E, L)

    def per_group(v):
        v3 = jnp.transpose(v.reshape(2, G, E), (1, 0, 2)).reshape(G, 2 * E)
        return v3[:, None, :], v3[:, :, None]

    dbc, dbr = per_group(dt_bias)
    alc, alr = per_group(a_log)
    x_blk0 = di // gw
    b_blk0 = (2 * di) // N
    c_blk0 = (2 * di + G * N) // N
    in_specs = [
        pl.BlockSpec((seq, gw), lambda b, g: (row_blk0 + b, g)),
        pl.BlockSpec((seq, gw), lambda b, g: (row_blk0 + b, x_blk0 + g)),
        pl.BlockSpec((seq, N), lambda b, g: (row_blk0 + b, b_blk0 + g)),
        pl.BlockSpec((seq, N), lambda b, g: (row_blk0 + b, c_blk0 + g)),
        pl.BlockSpec((SSD_CONV_W, gw), lambda b, g: (0, g)),
        pl.BlockSpec((SSD_CONV_W, N), lambda b, g: (0, di // N + g)),
        pl.BlockSpec((SSD_CONV_W, N), lambda b, g: (0, di // N + G + g)),
        pl.BlockSpec((1, gw), lambda b, g: (0, g)),
        pl.BlockSpec((1, N), lambda b, g: (0, di // N + g)),
        pl.BlockSpec((1, N), lambda b, g: (0, di // N + G + g)),
        pl.BlockSpec((None, None, seq, 2 * E), lambda b, g: (b, g, 0, 0)),
        pl.BlockSpec((None, None, nc, 2 * E, L), lambda b, g: (b, g, 0, 0, 0)),
        pl.BlockSpec((None, 1, 2 * E), lambda b, g: (g, 0, 0)),
        pl.BlockSpec((None, 2 * E, 1), lambda b, g: (g, 0, 0)),
        pl.BlockSpec((None, 1, 2 * E), lambda b, g: (g, 0, 0)),
        pl.BlockSpec((None, 2 * E, 1), lambda b, g: (g, 0, 0)),
        pl.BlockSpec((1, gw), lambda b, g: (0, g)),
        pl.BlockSpec((1, gw), lambda b, g: (0, g)),
    ]
    cb2 = conv_b.reshape(1, -1)
    args = [zxbc, zxbc, zxbc, zxbc, conv_w, conv_w, conv_w, cb2, cb2, cb2,
            dtc, dtr, dbc, dbr, alc, alr,
            jnp.repeat(d_skip, P).reshape(1, di), norm_g.reshape(1, di)]
    if init is not None:
        in_specs.append(pl.BlockSpec((None, 2, E, P, N), lambda b, g: (b, 0, g, 0, 0)))
        args.append(init)
    out_specs = [
        pl.BlockSpec((seq, gw), lambda b, g: (row_blk0 + b, g)),
        pl.BlockSpec((seq, 1), lambda b, g: (row_blk0 + b, 0)),
    ]
    out_shape = [jax.ShapeDtypeStruct((m, di), BF16), jax.ShapeDtypeStruct((m, 1), F32)]
    if state_out:
        out_specs.append(pl.BlockSpec((None, 2, E, P, N), lambda b, g: (b, 0, g, 0, 0)))
        out_shape.append(jax.ShapeDtypeStruct((batch, 2, heads, P, N), F32))
    return _call_aliased(
        functools.partial(_ssd_kernel, n_chunks=nc, has_init=init is not None, has_state_out=state_out),
        {} if prev is None else {0: prev[0], 1: prev[1]},
        grid=(batch, G),
        in_specs=in_specs,
        args=args,
        out_specs=out_specs,
        out_shape=out_shape,
        scratch_shapes=[
            pltpu.VMEM((seq + 2 * CONV_PAD_ROWS, gw + 2 * N), BF16),
            pltpu.VMEM((seq, gw), F32),
            pltpu.VMEM((seq, gw), BF16),
            pltpu.VMEM((nc, gw, L), F32),
            pltpu.VMEM((seq, N), BF16),
            pltpu.VMEM((seq, N), BF16),
            pltpu.VMEM((seq, 2 * E), F32),
            pltpu.VMEM((seq, gw), F32),
            pltpu.VMEM((seq, gw), F32),
            pltpu.VMEM((2, E * P, N), F32),
        ],
        compiler_params=_params("parallel", "arbitrary"),
        name="ssd_core",
    )


def kernel(x_prompt, x_sample, cache_attn_k, cache_attn_v, state_mlstm_C, state_mlstm_n, state_mlstm_m, state_ssd_h, c, c_ctx, mod_w, mod_b, ln_g, ln_b, ffn_w_in, ffn_w_out, mlstm_w_in, mlstm_b_gate, mlstm_norm_g, mlstm_w_out, attn_w_qkv, attn_q_norm, attn_k_norm, attn_w_out, ssd_w_in, ssd_conv_w, ssd_conv_b, ssd_dt_bias, ssd_A_log, ssd_D, ssd_norm_g, ssd_w_out):
    bp, sp, d = x_prompt.shape
    bd, sd, _ = x_sample.shape
    depth = mod_w.shape[0]
    n_ctx = bp * sp
    m = n_ctx + bd * sd
    rows = _Rows(n_ctx, sd, d)
    alpha = (2 * depth) ** 0.25
    ctx_blk0, dec_blk0 = 0, n_ctx // sd

    cvec = jnp.zeros((N_SEG_PAD, d), F32).at[0].set(c_ctx).at[1:1 + bd].set(c)
    modv = _adaln(cvec, mod_w, mod_b).reshape(depth * N_SEG_PAD * N_MOD, 1, d)
    lnv = jnp.stack([ln_g, ln_b], axis=2).reshape(depth * 3 * 2, 1, d)

    x = jnp.concatenate([x_prompt.reshape(n_ctx, d), x_sample.reshape(bd * sd, d)], axis=0)
    h = _modulate(x, modv, rows, 0, 0, 1)

    qk_w, v_w = 2 * ML_HEADS * ML_DK, ML_HEADS * ML_DV
    mlstm_wt = jnp.swapaxes(mlstm_w_in, 1, 2)

    ml_states, attn_kv, ssd_states = None, [], []
    for i in range(depth):
        kind, j = i % 3, i // 3
        act = _ffn_in(h, ffn_w_in, (i, 0))
        x, h = _out_ln(act, ffn_w_out, (i, 0), x, modv, lnv, rows, layer=i, k_gate=2, ln_idx=3 * i,
                       alpha=alpha, coef=0.5, nxt=(i, 3, 4), name="ffn_out")
        ssq = None
        if kind == 0:
            qkv = _proj(h, mlstm_wt, (j,), 0, qk_w + v_w, BF16, scale_first=ML_DK ** -0.5,
                        bn=ML_HEADS * ML_DK, w_t=True, name="mlstm_qkv")
            og = _proj(h, mlstm_wt, (j,), qk_w + v_w, v_w, BF16, w_t=True, name="mlstm_o")
            n_gate = 4 * ML_HEADS
            gates = _small_proj(h, mlstm_wt, (j,), (qk_w + 2 * v_w) // n_gate, n_gate, w_t=True, name="mlstm_gates")
            y, *ml_states = _mlstm_core(qkv, og, gates, mlstm_b_gate[j], mlstm_norm_g[j],
                                        batch=bp, seq=sp, row_blk0=ctx_blk0,
                                        state_out=(mlstm_w_in.shape[0], j, ml_states))
            (y,) = _mlstm_core(qkv, og, gates, mlstm_b_gate[j], mlstm_norm_g[j],
                               batch=bd, seq=sd, row_blk0=dec_blk0, y_prev=y,
                               init=(state_mlstm_C, state_mlstm_n, state_mlstm_m, j))
            w_out = mlstm_w_out
        elif kind == 1:
            qkv = _proj(h, attn_w_qkv, (j,), 0, attn_w_qkv.shape[2], BF16, name="attn_qkv")
            y, kc, vc = _attn_core(qkv, attn_q_norm[j], attn_k_norm[j], batch=bp, seq=sp, row_blk0=ctx_blk0)
            past = cache_attn_k.shape[2]
            (y,) = _attn_core(qkv, attn_q_norm[j], attn_k_norm[j], batch=bd, seq=sd, row_blk0=dec_blk0,
                              rope=_rope_tables(sd), y_prev=y,
                              ctx=(cache_attn_k[:, j].reshape(bd, past, -1), cache_attn_v[:, j].reshape(bd, past, -1)))
            attn_kv.append((kc.reshape(bp, sp, ATTN_KV_HEADS, ATTN_HD), vc.reshape(bp, sp, ATTN_KV_HEADS, ATTN_HD)))
            w_out = attn_w_out
        else:
            di = ssd_D.shape[1] * SSD_HD
            zxbc_w = 2 * di + 2 * SSD_GROUPS * SSD_N
            zxbc = _proj(h, ssd_w_in, (j,), 0, zxbc_w, BF16, name="ssd_zxbc")
            dt_raw = _small_proj(h, ssd_w_in, (j,), zxbc_w // LANES, LANES, name="ssd_dt")
            y, ssq, hst = _ssd_core(zxbc, dt_raw, ssd_conv_w[j], ssd_conv_b[j], ssd_dt_bias[j], ssd_A_log[j],
                                    ssd_D[j], ssd_norm_g[j], batch=bp, seq=sp, row_blk0=ctx_blk0, state_out=True)
            y, ssq = _ssd_core(zxbc, dt_raw, ssd_conv_w[j], ssd_conv_b[j], ssd_dt_bias[j], ssd_A_log[j],
                               ssd_D[j], ssd_norm_g[j], batch=bd, seq=sd, row_blk0=dec_blk0,
                               init=state_ssd_h[:, j], prev=(y, ssq))
            ssd_states.append(hst)
            w_out = ssd_w_out
        x, h = _out_ln(y, w_out, (j,), x, modv, lnv, rows, layer=i, k_gate=5, ln_idx=3 * i + 1,
                       alpha=alpha, coef=1.0, nxt=(i, 6, 7), ssq=ssq,
                       ssq_dim=None if ssq is None else w_out.shape[1], bk=1024, name="mix_out")
        act = _ffn_in(h, ffn_w_in, (i, 1))
        nxt = (i + 1, 0, 1) if i + 1 < depth else None
        x, h = _out_ln(act, ffn_w_out, (i, 1), x, modv, lnv, rows, layer=i, k_gate=8, ln_idx=3 * i + 2,
                       alpha=alpha, coef=0.5, nxt=nxt, name="ffn_out")

    y_prompt = x[:n_ctx].reshape(bp, sp, d)
    y_sample = x[n_ctx:].reshape(bd, sd, d)
    new_k = jnp.stack([kv[0] for kv in attn_kv], axis=1)
    new_v = jnp.stack([kv[1] for kv in attn_kv], axis=1)
    new_c = ml_states[0]
    new_n = ml_states[1].reshape(ml_states[1].shape[:4] + (ML_DK,))
    new_m = ml_states[2].reshape(ml_states[2].shape[:4])
    new_h = jnp.stack(ssd_states, axis=1)
    return (y_prompt, y_sample, new_k, new_v, new_c, new_n, new_m, new_h)
```

```python
import functools
import math

import jax
import jax.numpy as jnp
from jax import lax
from jax.experimental import pallas as pl
from jax.experimental.pallas import tpu as pltpu

F32 = jnp.float32
BF16 = jnp.bfloat16

LN_EPS = 1e-6
N_MOD = 9
N_SEG_PAD = 8
LANES = 128
VMEM_LIMIT = 60 * 1024 * 1024
LN_ROWS = 256
LN_UNROLL = 1

ML_HEADS, ML_DK, ML_DV, ML_CHUNK = 8, 128, 256, 128
ML_HEADS_PER_STEP = 4
ATTN_HEADS, ATTN_KV_HEADS, ATTN_HD = 16, 4, 128
ATTN_GROUP = ATTN_HEADS // ATTN_KV_HEADS
ATTN_QBLOCK = 256
GRID_W = 64
ROPE_THETA = 10000.0
ROPE_PAIRS = ATTN_HD // 4
SSD_HD, SSD_GROUPS, SSD_N, SSD_CONV_W, SSD_CHUNK = 64, 8, 128, 5, 128
CONV_PAD_ROWS = 16


def _params(*sem):
    return pltpu.CompilerParams(dimension_semantics=sem, vmem_limit_bytes=VMEM_LIMIT)


def _lead_spec(lead, block, index_fn):
    lead = tuple(lead)
    return pl.BlockSpec((None,) * len(lead) + tuple(block), lambda *g: lead + tuple(index_fn(*g)))


def _call_aliased(kernel_fn, prev, *, in_specs, args, out_specs, out_shape, **kw):
    prev = dict(prev or {})
    n = len(prev)

    def body(*refs):
        return kernel_fn(*refs[n:])

    return pl.pallas_call(
        body,
        in_specs=[pl.BlockSpec(memory_space=pl.ANY)] * n + list(in_specs),
        out_specs=out_specs,
        out_shape=out_shape,
        input_output_aliases={i: o for i, o in enumerate(prev)},
        **kw,
    )(*prev.values(), *args)


def _silu(x):
    return x * jax.nn.sigmoid(x)


def _log_sigmoid(x):
    return jnp.minimum(x, 0.0) - jnp.log1p(jnp.exp(-jnp.abs(x)))


def _softplus(x):
    return jnp.maximum(x, 0.0) + jnp.log1p(jnp.exp(-jnp.abs(x)))


def _split_bf16(x, n):
    parts, r = [], x
    for _ in range(n):
        p = r.astype(BF16)
        parts.append(p)
        r = r - p.astype(F32)
    return parts


def _split3(x):
    return _split_bf16(x, 3)


def _split2(x):
    return _split_bf16(x, 2)


def _dot_split(parts, mask_bf, *, lhs_is_mask):
    acc = None
    for p in reversed(parts):
        t = (jnp.dot(mask_bf, p, preferred_element_type=F32) if lhs_is_mask
             else jnp.dot(p, mask_bf, preferred_element_type=F32))
        acc = t if acc is None else acc + t
    return acc


def _adaln_kernel(cv_ref, w_ref, b_ref, o_ref):
    s = _silu(cv_ref[...]).astype(BF16)
    w = w_ref[...].astype(BF16)
    o_ref[...] = jnp.dot(s, w, preferred_element_type=F32) + b_ref[...]


def _adaln(cvec, mod_w, mod_b, bn=1024):
    depth, d, n = mod_w.shape
    return pl.pallas_call(
        _adaln_kernel,
        grid=(depth, n // bn),
        in_specs=[
            pl.BlockSpec((N_SEG_PAD, d), lambda l, j: (0, 0)),
            pl.BlockSpec((None, d, bn), lambda l, j: (l, 0, j)),
            pl.BlockSpec((None, 1, bn), lambda l, j: (l, 0, j)),
        ],
        out_specs=pl.BlockSpec((None, N_SEG_PAD, bn), lambda l, j: (l, 0, j)),
        out_shape=jax.ShapeDtypeStruct((depth, N_SEG_PAD, n), F32),
        compiler_params=_params("arbitrary", "arbitrary"),
        name="adaln",
    )(cvec, mod_w, mod_b.reshape(depth, 1, n))


class _Rows:
    def __init__(self, n_ctx_rows, dec_seq, d_model):
        self.n_ctx = n_ctx_rows
        self.dec_seq = dec_seq
        self.d = d_model

    def seg(self, i, bm):
        r = i * bm
        return jnp.where(r < self.n_ctx, 0, 1 + (r - self.n_ctx) // self.dec_seq)


def _mod_spec(rows, layer, k, bm):
    def idx(i, *_):
        return ((layer * N_SEG_PAD + rows.seg(i, bm)) * N_MOD + k, 0, 0)

    return pl.BlockSpec((None, 1, rows.d), idx)


def _modulate_kernel(x_ref, sh_ref, sc_ref, h_ref):
    h_ref[...] = (x_ref[...] * (1.0 + sc_ref[...]) + sh_ref[...]).astype(BF16)


def _modulate(x, modv, rows, layer, k_shift, k_scale, bm=512):
    m, d = x.shape
    return pl.pallas_call(
        _modulate_kernel,
        grid=(m // bm,),
        in_specs=[
            pl.BlockSpec((bm, d), lambda i: (i, 0)),
            _mod_spec(rows, layer, k_shift, bm),
            _mod_spec(rows, layer, k_scale, bm),
        ],
        out_specs=pl.BlockSpec((bm, d), lambda i: (i, 0)),
        out_shape=jax.ShapeDtypeStruct((m, d), BF16),
        compiler_params=_params("parallel"),
        name="modulate",
    )(x, modv, modv)


def _ffn_in_kernel(a_ref, wg_ref, wu_ref, o_ref, w_scr):
    bn = wg_ref.shape[1]

    @pl.when(pl.program_id(1) == 0)
    def _():
        w_scr[:, :bn] = wg_ref[...].astype(BF16)
        w_scr[:, bn:] = wu_ref[...].astype(BF16)

    u = jnp.dot(a_ref[...], w_scr[...], preferred_element_type=F32)
    o_ref[...] = (_silu(u[:, :bn]) * u[:, bn:]).astype(BF16)


def _ffn_in(a, w_in, lead, bm=1024, bn=512):
    m, k = a.shape
    f = w_in.shape[-1] // 2
    nt = f // bn
    return pl.pallas_call(
        _ffn_in_kernel,
        grid=(nt, m // bm),
        in_specs=[
            pl.BlockSpec((bm, k), lambda j, i: (i, 0)),
            _lead_spec(lead, (k, bn), lambda j, i: (0, j)),
            _lead_spec(lead, (k, bn), lambda j, i: (0, j + nt)),
        ],
        out_specs=pl.BlockSpec((bm, bn), lambda j, i: (i, j)),
        out_shape=jax.ShapeDtypeStruct((m, f), BF16),
        scratch_shapes=[pltpu.VMEM((k, 2 * bn), BF16)],
        compiler_params=_params("arbitrary", "arbitrary"),
        name="ffn_in",
    )(a, w_in, w_in)


def _proj_kernel(a_ref, w_ref, o_ref, w_scr, *, scale_first, w_t):
    @pl.when(pl.program_id(1) == 0)
    def _():
        w = w_ref[...]
        w_scr[...] = (jnp.transpose(w) if w_t else w).astype(BF16)

    u = jnp.dot(a_ref[...], w_scr[...], preferred_element_type=F32)
    if scale_first is not None:
        u = u * jnp.where(pl.program_id(0) == 0, scale_first, 1.0)
    o_ref[...] = u.astype(o_ref.dtype)


def _proj(a, w, lead, col0, ncols, out_dtype, bm=2048, bn=1024, scale_first=None, w_t=False, name="proj"):
    m, k = a.shape
    assert col0 % bn == 0 and ncols % bn == 0
    j0 = col0 // bn
    w_spec = (_lead_spec(lead, (bn, k), lambda j, i: (j + j0, 0)) if w_t
              else _lead_spec(lead, (k, bn), lambda j, i: (0, j + j0)))
    return pl.pallas_call(
        functools.partial(_proj_kernel, scale_first=scale_first, w_t=w_t),
        grid=(ncols // bn, m // bm),
        in_specs=[
            pl.BlockSpec((bm, k), lambda j, i: (i, 0)),
            w_spec,
        ],
        out_specs=pl.BlockSpec((bm, bn), lambda j, i: (i, j)),
        out_shape=jax.ShapeDtypeStruct((m, ncols), out_dtype),
        scratch_shapes=[pltpu.VMEM((k, bn), BF16)],
        compiler_params=_params("arbitrary", "arbitrary"),
        name=name,
    )(a, w)


def _small_proj_kernel(a_ref, w_ref, o_ref, *, w_t):
    w = w_ref[...].astype(BF16)
    if w_t:
        o_ref[...] = lax.dot_general(a_ref[...], w, (((1,), (1,)), ((), ())), preferred_element_type=F32)
    else:
        o_ref[...] = jnp.dot(a_ref[...], w, preferred_element_type=F32)


def _small_proj(a, w, lead, col_block, ncols, bm=1024, w_t=False, name="small_proj"):
    m, k = a.shape
    w_spec = (_lead_spec(lead, (ncols, k), lambda i: (col_block, 0)) if w_t
              else _lead_spec(lead, (k, ncols), lambda i: (0, col_block)))
    return pl.pallas_call(
        functools.partial(_small_proj_kernel, w_t=w_t),
        grid=(m // bm,),
        in_specs=[
            pl.BlockSpec((bm, k), lambda i: (i, 0)),
            w_spec,
        ],
        out_specs=pl.BlockSpec((bm, ncols), lambda i: (i, 0)),
        out_shape=jax.ShapeDtypeStruct((m, ncols), F32),
        compiler_params=_params("parallel"),
        name=name,
    )(a, w)


def _out_ln_kernel(*refs, alpha, coef, has_next, has_ssq, ssq_dim, x_dma):
    it = iter(refs)
    a_ref, w_ref, x_ref, gate_ref, g_ref, b_ref = (next(it) for _ in range(6))
    sh_ref = sc_ref = ssq_ref = None
    if has_next:
        sh_ref, sc_ref = next(it), next(it)
    if has_ssq:
        ssq_ref = next(it)
    xo_ref = next(it)
    ho_ref = next(it) if has_next else None

    k = pl.program_id(1)

    if x_dma:
        x_hbm, x_ref, x_sem = x_ref, next(it), next(it)
        bm = xo_ref.shape[0]

        def x_copy():
            r0 = pl.multiple_of(pl.program_id(0) * bm, bm)
            return pltpu.make_async_copy(x_hbm.at[pl.ds(r0, bm), :], x_ref, x_sem)

    @pl.when(k == 0)
    def _():
        if x_dma:
            x_copy().start()
        xo_ref[...] = jnp.dot(a_ref[...], w_ref[...].astype(BF16), preferred_element_type=F32)

    @pl.when(k != 0)
    def _():
        xo_ref[...] += jnp.dot(a_ref[...], w_ref[...].astype(BF16), preferred_element_type=F32)

    @pl.when(k == pl.num_programs(1) - 1)
    def _():
        if x_dma:
            x_copy().wait()
        d_model = xo_ref.shape[1]
        tiles = [slice(t * LANES, (t + 1) * LANES) for t in range(d_model // LANES)]

        def slab(si, carry):
            rs = pl.ds(pl.multiple_of(si * LN_ROWS, LN_ROWS), LN_ROWS)
            r = lax.rsqrt(ssq_ref[rs, :] * (1.0 / ssq_dim) + LN_EPS) if has_ssq else None
            acc = None
            for cs in tiles:
                y = xo_ref[rs, cs]
                if has_ssq:
                    y = y * r
                z = alpha * x_ref[rs, cs] + (coef * gate_ref[:, cs]) * y
                xo_ref[rs, cs] = z
                acc = z if acc is None else acc + z
            mu = jnp.sum(acc, axis=1, keepdims=True) * (1.0 / d_model)
            acc = None
            for cs in tiles:
                zc = xo_ref[rs, cs] - mu
                xo_ref[rs, cs] = zc
                acc = zc * zc if acc is None else acc + zc * zc
            rstd = lax.rsqrt(jnp.sum(acc, axis=1, keepdims=True) * (1.0 / d_model) + LN_EPS)
            for cs in tiles:
                xn = xo_ref[rs, cs] * rstd * g_ref[:, cs] + b_ref[:, cs]
                xo_ref[rs, cs] = xn
                if has_next:
                    ho_ref[rs, cs] = (xn * (1.0 + sc_ref[:, cs]) + sh_ref[:, cs]).astype(BF16)
            return carry

        lax.fori_loop(0, xo_ref.shape[0] // LN_ROWS, slab, 0, unroll=LN_UNROLL)


def _out_ln(a, w, lead, x, modv, lnv, rows, *, layer, k_gate, ln_idx, alpha, coef,
            nxt=None, ssq=None, ssq_dim=None, bm=1024, bk=512, x_dma=False, name="out_ln"):
    m, kdim = a.shape
    d = x.shape[1]
    assert kdim % bk == 0
    assert not x_dma or kdim // bk >= 2
    in_specs = [
        pl.BlockSpec((bm, bk), lambda i, k: (i, k)),
        _lead_spec(lead, (bk, d), lambda i, k: (k, 0)),
        pl.BlockSpec(memory_space=pl.ANY) if x_dma else pl.BlockSpec((bm, d), lambda i, k: (i, 0)),
        _mod_spec(rows, layer, k_gate, bm),
        pl.BlockSpec((None, 1, d), lambda i, k: (2 * ln_idx, 0, 0)),
        pl.BlockSpec((None, 1, d), lambda i, k: (2 * ln_idx + 1, 0, 0)),
    ]
    args = [a, w, x, modv, lnv, lnv]
    if nxt is not None:
        in_specs += [_mod_spec(rows, nxt[0], nxt[1], bm), _mod_spec(rows, nxt[0], nxt[2], bm)]
        args += [modv, modv]
    if ssq is not None:
        in_specs.append(pl.BlockSpec((bm, ssq.shape[1]), lambda i, k: (i, 0)))
        args.append(ssq)
    out_specs = [pl.BlockSpec((bm, d), lambda i, k: (i, 0))]
    out_shape = [jax.ShapeDtypeStruct((m, d), F32)]
    if nxt is not None:
        out_specs.append(pl.BlockSpec((bm, d), lambda i, k: (i, 0)))
        out_shape.append(jax.ShapeDtypeStruct((m, d), BF16))
    res = pl.pallas_call(
        functools.partial(_out_ln_kernel, alpha=alpha, coef=coef, has_next=nxt is not None,
                          has_ssq=ssq is not None, ssq_dim=ssq_dim, x_dma=x_dma),
        grid=(m // bm, kdim // bk),
        in_specs=in_specs,
        out_specs=out_specs,
        out_shape=out_shape,
        scratch_shapes=[pltpu.VMEM((bm, d), F32), pltpu.SemaphoreType.DMA(())] if x_dma else [],
        compiler_params=_params("arbitrary", "arbitrary"),
        name=name,
    )(*args)
    return (res[0], res[1]) if nxt is not None else (res[0], None)


def _mlstm_kernel(*refs, n_chunks, hp, has_init, has_state_out):
    it = iter(refs)
    q_ref, k_ref, v_ref, o_ref, gr_ref, br_ref, ng_ref = (next(it) for _ in range(7))
    c0_ref = n0_ref = m0_ref = None
    if has_init:
        c0_ref, n0_ref, m0_ref = next(it), next(it), next(it)
    y_ref = next(it)
    co_ref = no_ref = mo_ref = None
    if has_state_out:
        co_ref, no_ref, mo_ref = next(it), next(it), next(it)
    hf_scr, hb_scr, kt_scr, vx_scr, cn_scr, m_scr = (next(it) for _ in range(6))

    L, dk, dv = ML_CHUNK, ML_DK, ML_DV
    t_idx = lax.broadcasted_iota(jnp.int32, (L, L), 0)
    s_idx = lax.broadcasted_iota(jnp.int32, (L, L), 1)
    ones_bf = jnp.ones((L, LANES), BF16)
    masks = []
    for d in range(2):
        incl = (s_idx <= t_idx) if d == 0 else (s_idx >= t_idx)
        incl_t = (t_idx <= s_idx) if d == 0 else (t_idx >= s_idx)
        masks.append((incl, jnp.where(incl_t, 1.0, 0.0).astype(BF16)))

    for hh in range(hp):
        vx_scr[:, hh * (dv + LANES):hh * (dv + LANES) + dv] = v_ref[:, hh * dv:(hh + 1) * dv]
        vx_scr[:, hh * (dv + LANES) + dv:(hh + 1) * (dv + LANES)] = jnp.ones((v_ref.shape[0], LANES), BF16)
        for c in range(n_chunks):
            kt_scr[c, hh * dk:(hh + 1) * dk, :] = jnp.transpose(
                k_ref[c * L:(c + 1) * L, hh * dk:(hh + 1) * dk].astype(F32))
        for d in range(2):
            i = hh * 2 + d
            if has_init:
                cn_scr[i, :, :dv] = c0_ref[d, hh]
                cn_scr[i, :, dv:] = jnp.transpose(jnp.broadcast_to(n0_ref[d, hh], (LANES, dk)))
                m_scr[i] = m0_ref[d, hh]
            else:
                cn_scr[i] = jnp.zeros((dk, dv + LANES), F32)
                m_scr[i] = jnp.zeros((1, 1), F32)

    def step(ci, carry):
        chains = []
        for d in range(2):
            incl, incl_t_bf = masks[d]
            c = ci if d == 0 else n_chunks - 1 - ci
            r0 = pl.multiple_of(c * L, L)
            last = L - 1 if d == 0 else 0
            grow = gr_ref[c] + br_ref[...]
            lf_parts = _split3(_log_sigmoid(grow))
            cum_r = _dot_split(lf_parts, incl_t_bf, lhs_is_mask=False)
            lf_parts = [p.astype(F32) for p in lf_parts]
            for hh in range(hp):
                gi, gf = hh * 4 + d, hh * 4 + 2 + d
                ch = dict(d=d, hh=hh, i=hh * 2 + d, c=c, r0=r0, incl=incl,
                          li_r=grow[gi:gi + 1, :], b_r=cum_r[gf:gf + 1, :],
                          lf_rows=[p[gf:gf + 1, :] for p in lf_parts])
                ch["g"] = ch["b_r"][:, last:last + 1]
                chains.append(ch)

        for ch in chains:
            b_c = None
            for p in reversed(ch["lf_rows"]):
                t = jnp.dot(jnp.where(ch["incl"], p, 0.0).astype(BF16), ones_bf, preferred_element_type=F32)
                b_c = t if b_c is None else b_c + t
            ch["b_c"] = b_c
        for ch in chains:
            hh, r0 = ch["hh"], ch["r0"]
            qc = q_ref[pl.ds(r0, L), hh * dk:(hh + 1) * dk]
            kc = k_ref[pl.ds(r0, L), hh * dk:(hh + 1) * dk]
            ch["cn_prev"] = cn_scr[ch["i"]]
            ch["m_prev"] = m_scr[ch["i"]]
            ch["qk"] = lax.dot_general(qc, kc, (((1,), (1,)), ((), ())), preferred_element_type=F32)
            ch["q_cn"] = jnp.dot(qc, ch["cn_prev"].astype(BF16), preferred_element_type=F32)
        for ch in chains:
            hh, r0 = ch["hh"], ch["r0"]
            dmat = jnp.where(ch["incl"], ch["b_c"] - ch["b_r"] + ch["li_r"], -jnp.inf)
            inter = ch["b_c"] + ch["m_prev"]
            m_t = jnp.maximum(inter, jnp.max(dmat, axis=1, keepdims=True))
            s_bf = (ch["qk"] * jnp.exp(dmat - m_t)).astype(BF16)
            ch["sc"] = jnp.exp(inter - m_t)
            ch["floor"] = jnp.exp(-m_t)
            ch["vc"] = vx_scr[pl.ds(r0, L), hh * (dv + LANES):(hh + 1) * (dv + LANES)]
            ch["s_v"] = jnp.dot(s_bf, ch["vc"], preferred_element_type=F32)
        for ch in chains:
            hh, r0, sc, q_cn, s_v = ch["hh"], ch["r0"], ch["sc"], ch["q_cn"], ch["s_v"]
            hs_scr = hf_scr if ch["d"] == 0 else hb_scr
            den = sc * q_cn[:, dv:] + s_v[:, dv:]
            inv = 1.0 / jnp.maximum(jnp.abs(den), ch["floor"])
            for cb in range(dv // LANES):
                cs = slice(cb * LANES, (cb + 1) * LANES)
                hs_scr[pl.ds(r0, L), hh * dv + cb * LANES:hh * dv + (cb + 1) * LANES] = (
                    sc * q_cn[:, cs] + s_v[:, cs]) * inv
        for ch in chains:
            hh, g, m_prev = ch["hh"], ch["g"], ch["m_prev"]
            ds_r = g - ch["b_r"] + ch["li_r"]
            m_new = jnp.maximum(g + m_prev, jnp.max(ds_r, axis=1, keepdims=True))
            ktw = (kt_scr[ch["c"], hh * dk:(hh + 1) * dk, :] * jnp.exp(ds_r - m_new)).astype(BF16)
            decay = jnp.exp(g + m_prev - m_new)
            cn_scr[ch["i"]] = decay * ch["cn_prev"] + jnp.dot(ktw, ch["vc"], preferred_element_type=F32)
            m_scr[ch["i"]] = m_new
        return carry

    lax.fori_loop(0, n_chunks, step, 0)

    for hh in range(hp):
        if has_state_out:
            for d in range(2):
                co_ref[d, hh] = cn_scr[hh * 2 + d, :, :dv]
                no_ref[d, hh] = jnp.transpose(cn_scr[hh * 2 + d, :, dv:])[0:1, :]
                mo_ref[d, hh] = m_scr[hh * 2 + d]
        cols = slice(hh * dv, (hh + 1) * dv)
        hs = hf_scr[:, cols] + hb_scr[:, cols]
        mu = jnp.mean(hs, axis=1, keepdims=True)
        hc = hs - mu
        var = jnp.mean(hc * hc, axis=1, keepdims=True)
        hn = hc * lax.rsqrt(var + LN_EPS) * ng_ref[:, cols]
        y_ref[:, cols] = (jax.nn.sigmoid(o_ref[:, cols].astype(F32)) * hn).astype(BF16)


def _mlstm_core(qkv, o, gates, b_gate, norm_g, *, batch, seq, row_blk0, init=None, state_out=None, y_prev=None):
    m = qkv.shape[0]
    h, dk, dv, L, hp = ML_HEADS, ML_DK, ML_DV, ML_CHUNK, ML_HEADS_PER_STEP
    hg = h // hp
    nc = seq // L
    r0 = row_blk0 * seq
    g4 = jnp.swapaxes(gates[r0:r0 + batch * seq].reshape(batch, seq, 4, h), 2, 3)
    grow = jnp.transpose(g4.reshape(batch, nc, L, hg, hp * 4), (0, 3, 1, 4, 2))
    brow = jnp.transpose(b_gate.reshape(4, h)).reshape(hg, hp * 4, 1)
    kq = (h * dk) // (hp * dk)
    kv = (2 * h * dk) // (hp * dv)
    in_specs = [
        pl.BlockSpec((seq, hp * dk), lambda b, g: (row_blk0 + b, g)),
        pl.BlockSpec((seq, hp * dk), lambda b, g: (row_blk0 + b, kq + g)),
        pl.BlockSpec((seq, hp * dv), lambda b, g: (row_blk0 + b, kv + g)),
        pl.BlockSpec((seq, hp * dv), lambda b, g: (row_blk0 + b, g)),
        pl.BlockSpec((None, None, nc, hp * 4, L), lambda b, g: (b, g, 0, 0, 0)),
        pl.BlockSpec((None, hp * 4, 1), lambda b, g: (g, 0, 0)),
        pl.BlockSpec((1, hp * dv), lambda b, g: (0, g)),
    ]
    args = [qkv, qkv, qkv, o, grow, brow, norm_g.reshape(1, h * dv)]
    if init is not None:
        c0, n0, m0, lyr = init
        nl = c0.shape[1]
        in_specs += [
            pl.BlockSpec((None, None, 2, hp, dk, dv), lambda b, g: (b, lyr, 0, g, 0, 0)),
            pl.BlockSpec((None, None, 2, hp, 1, dk), lambda b, g: (b, lyr, 0, g, 0, 0)),
            pl.BlockSpec((None, None, 2, hp, 1, 1), lambda b, g: (b, lyr, 0, g, 0, 0)),
        ]
        args += [c0, n0.reshape(batch, nl, 2, h, 1, dk), m0.reshape(batch, nl, 2, h, 1, 1)]
    out_specs = [pl.BlockSpec((seq, hp * dv), lambda b, g: (row_blk0 + b, g))]
    out_shape = [jax.ShapeDtypeStruct((m, h * dv), BF16)]
    prev = {} if y_prev is None else {0: y_prev}
    if state_out is not None:
        nl_out, slot, st_prev = state_out
        out_specs += [
            pl.BlockSpec((None, None, 2, hp, dk, dv), lambda b, g: (b, slot, 0, g, 0, 0)),
            pl.BlockSpec((None, None, 2, hp, 1, dk), lambda b, g: (b, slot, 0, g, 0, 0)),
            pl.BlockSpec((None, None, 2, hp, 1, 1), lambda b, g: (b, slot, 0, g, 0, 0)),
        ]
        out_shape += [
            jax.ShapeDtypeStruct((batch, nl_out, 2, h, dk, dv), F32),
            jax.ShapeDtypeStruct((batch, nl_out, 2, h, 1, dk), F32),
            jax.ShapeDtypeStruct((batch, nl_out, 2, h, 1, 1), F32),
        ]
        if st_prev is not None:
            prev.update({1: st_prev[0], 2: st_prev[1], 3: st_prev[2]})
    return _call_aliased(
        functools.partial(_mlstm_kernel, n_chunks=nc, hp=hp, has_init=init is not None,
                          has_state_out=state_out is not None),
        prev,
        grid=(batch, hg),
        in_specs=in_specs,
        args=args,
        out_specs=out_specs,
        out_shape=out_shape,
        scratch_shapes=[
            pltpu.VMEM((seq, hp * dv), F32),
            pltpu.VMEM((seq, hp * dv), F32),
            pltpu.VMEM((nc, hp * dk, L), F32),
            pltpu.VMEM((seq, hp * (dv + LANES)), BF16),
            pltpu.VMEM((2 * hp, dk, dv + LANES), F32),
            pltpu.VMEM((2 * hp, 1, 1), F32),
        ],
        compiler_params=_params("parallel", "parallel"),
        name="mlstm_core",
    )


def _rope(x, cos_t, sin_t):
    lane = lax.broadcasted_iota(jnp.int32, x.shape, 1)
    first = (lane % (2 * ROPE_PAIRS)) < ROPE_PAIRS
    rot = jnp.where(first, pltpu.roll(x, x.shape[1] - ROPE_PAIRS, 1), pltpu.roll(x, ROPE_PAIRS, 1))
    return x * cos_t + rot * sin_t


def _rms(x, g):
    return x * lax.rsqrt(jnp.mean(x * x, axis=1, keepdims=True) + LN_EPS) * g


def _attn_kernel(*refs, seq, has_ctx):
    it = iter(refs)
    q_ref, k_ref, v_ref, qg_ref, kg_ref = (next(it) for _ in range(5))
    cos_ref = sin_ref = ck_ref = cv_ref = None
    if has_ctx:
        cos_ref, sin_ref, ck_ref, cv_ref = (next(it) for _ in range(4))
    o_ref = next(it)
    ko_ref = vo_ref = None
    if not has_ctx:
        ko_ref, vo_ref = next(it), next(it)
    keys_scr, vals_scr = next(it), next(it)

    hd = ATTN_HD
    kn = _rms(k_ref[...].astype(F32), kg_ref[...])
    if has_ctx:
        keys_scr[:seq, :] = _rope(kn, cos_ref[...], sin_ref[...]).astype(BF16)
        keys_scr[seq:, :] = ck_ref[...].astype(BF16)
        vals_scr[:seq, :] = v_ref[...]
        vals_scr[seq:, :] = cv_ref[...].astype(BF16)
    else:
        ko_ref[...] = kn
        vo_ref[...] = v_ref[...].astype(F32)
        keys_scr[...] = kn.astype(BF16)
        vals_scr[...] = v_ref[...]

    qb = min(ATTN_QBLOCK, seq)
    scale = hd ** -0.5

    def qblock(bi, carry):
        r0 = pl.multiple_of(bi * qb, qb)
        ss = []
        for g in range(ATTN_GROUP):
            q = _rms(q_ref[pl.ds(r0, qb), g * hd:(g + 1) * hd].astype(F32), qg_ref[...])
            if has_ctx:
                q = _rope(q, cos_ref[pl.ds(r0, qb), :], sin_ref[pl.ds(r0, qb), :])
            q = (q * scale).astype(BF16)
            ss.append(lax.dot_general(q, keys_scr[...], (((1,), (1,)), ((), ())), preferred_element_type=F32))
        ps, dens = [], []
        for s in ss:
            p = jnp.exp(s - jnp.max(s, axis=1, keepdims=True))
            dens.append(jnp.sum(p, axis=1, keepdims=True))
            ps.append(p.astype(BF16))
        for g in range(ATTN_GROUP):
            o = jnp.dot(ps[g], vals_scr[...], preferred_element_type=F32) / dens[g]
            o_ref[pl.ds(r0, qb), g * hd:(g + 1) * hd] = o.astype(BF16)
        return carry

    lax.fori_loop(0, seq // qb, qblock, 0)


def _attn_core(qkv, q_g, k_g, *, batch, seq, row_blk0, rope=None, ctx=None, y_prev=None):
    m = qkv.shape[0]
    hd, kvh, grp = ATTN_HD, ATTN_KV_HEADS, ATTN_GROUP
    has_ctx = ctx is not None
    k_off = ATTN_HEADS
    v_off = ATTN_HEADS + kvh
    in_specs = [
        pl.BlockSpec((seq, grp * hd), lambda b, kv: (row_blk0 + b, kv)),
        pl.BlockSpec((seq, hd), lambda b, kv: (row_blk0 + b, k_off + kv)),
        pl.BlockSpec((seq, hd), lambda b, kv: (row_blk0 + b, v_off + kv)),
        pl.BlockSpec((1, hd), lambda b, kv: (0, 0)),
        pl.BlockSpec((1, hd), lambda b, kv: (0, 0)),
    ]
    args = [qkv, qkv, qkv, q_g.reshape(1, hd), k_g.reshape(1, hd)]
    n_keys = seq
    if has_ctx:
        cos_t, sin_t = rope
        ck, cv = ctx
        past = ck.shape[1]
        n_keys = seq + past
        in_specs += [
            pl.BlockSpec((seq, hd), lambda b, kv: (0, 0)),
            pl.BlockSpec((seq, hd), lambda b, kv: (0, 0)),
            pl.BlockSpec((None, past, hd), lambda b, kv: (b, 0, kv)),
            pl.BlockSpec((None, past, hd), lambda b, kv: (b, 0, kv)),
        ]
        args += [cos_t, sin_t, ck, cv]
    out_specs = [pl.BlockSpec((seq, grp * hd), lambda b, kv: (row_blk0 + b, kv))]
    out_shape = [jax.ShapeDtypeStruct((m, ATTN_HEADS * hd), BF16)]
    if not has_ctx:
        out_specs += [pl.BlockSpec((None, seq, hd), lambda b, kv: (b, 0, kv))] * 2
        out_shape += [jax.ShapeDtypeStruct((batch, seq, kvh * hd), F32)] * 2
    return _call_aliased(
        functools.partial(_attn_kernel, seq=seq, has_ctx=has_ctx),
        {} if y_prev is None else {0: y_prev},
        grid=(batch, kvh),
        in_specs=in_specs,
        args=args,
        out_specs=out_specs,
        out_shape=out_shape,
        scratch_shapes=[pltpu.VMEM((n_keys, hd), BF16), pltpu.VMEM((n_keys, hd), BF16)],
        compiler_params=_params("parallel", "parallel"),
        name="attn_core",
    )


def _rope_tables(seq):
    rows = seq // GRID_W
    row = jnp.repeat(jnp.arange(rows, dtype=F32), GRID_W)
    col = jnp.tile(jnp.arange(GRID_W, dtype=F32), rows)
    freqs = ROPE_THETA ** (-jnp.arange(ROPE_PAIRS, dtype=F32) / ROPE_PAIRS)
    ar, ac = row[:, None] * freqs, col[:, None] * freqs
    cos_t = jnp.concatenate([jnp.cos(ar), jnp.cos(ar), jnp.cos(ac), jnp.cos(ac)], axis=1)
    sin_t = jnp.concatenate([-jnp.sin(ar), jnp.sin(ar), -jnp.sin(ac), jnp.sin(ac)], axis=1)
    return cos_t, sin_t


def _dwconv_silu_chunk(xpad_ref, r0, w, b):
    L, padr = SSD_CHUNK, CONV_PAD_ROWS
    win = xpad_ref[pl.ds(r0, L + 2 * padr), :]
    t_idx = lax.broadcasted_iota(jnp.int32, (L, L + 2 * padr), 0)
    s_idx = lax.broadcasted_iota(jnp.int32, (L, L + 2 * padr), 1)
    acc = b
    for j in range(SSD_CONV_W):
        off = j - SSD_CONV_W // 2
        if off == 0:
            xs = win[padr:padr + L, :].astype(F32)
        else:
            shift = jnp.where(s_idx == t_idx + (padr + off), 1.0, 0.0).astype(BF16)
            xs = jnp.dot(shift, win, preferred_element_type=F32)
        acc = acc + xs * w[j:j + 1, :]
    return _silu(acc)


def _ssd_kernel(*refs, n_chunks, has_init, has_state_out):
    it = iter(refs)
    (z_ref, x_ref, b_ref, c_ref, wx_ref, wb_ref, wc_ref, bx_ref, bb_ref, bc_ref,
     dtc_ref, dtr_ref, dbc_ref, dbr_ref, alc_ref, alr_ref, dsk_ref, ng_ref) = (next(it) for _ in range(18))
    h0_ref = next(it) if has_init else None
    y_ref, ssq_ref = next(it), next(it)
    ho_ref = next(it) if has_state_out else None
    (xpad_scr, xs_scr, xsb_scr, xst_scr, bm_scr, cm_scr, dtc_scr,
     yf_scr, yb_scr, h_scr) = (next(it) for _ in range(10))

    L, P, N = SSD_CHUNK, SSD_HD, SSD_N
    gw = x_ref.shape[1]
    E = gw // P
    n_pairs = gw // LANES
    seq, padr = x_ref.shape[0], CONV_PAD_ROWS

    cw = gw + 2 * N
    xpad_scr[0:padr, :] = jnp.zeros((padr, cw), BF16)
    xpad_scr[padr + seq:, :] = jnp.zeros((padr, cw), BF16)
    xpad_scr[padr:padr + seq, 0:gw] = x_ref[...]
    xpad_scr[padr:padr + seq, gw:gw + N] = b_ref[...]
    xpad_scr[padr:padr + seq, gw + N:] = c_ref[...]
    conv_w = jnp.concatenate([wx_ref[...], wb_ref[...], wc_ref[...]], axis=1)
    conv_b = jnp.concatenate([bx_ref[...], bb_ref[...], bc_ref[...]], axis=1)
    for c in range(n_chunks):
        rows = slice(c * L, (c + 1) * L)
        xbc = _dwconv_silu_chunk(xpad_scr, c * L, conv_w, conv_b)
        xs_scr[rows, :] = xbc[:, :gw]
        xsb_scr[rows, :] = xbc[:, :gw].astype(BF16)
        bm_scr[rows, :] = xbc[:, gw:gw + N].astype(BF16)
        cm_scr[rows, :] = xbc[:, gw + N:].astype(BF16)
        for j in range(n_pairs):
            xst_scr[c, j * LANES:(j + 1) * LANES, :] = jnp.transpose(xbc[:, j * LANES:(j + 1) * LANES])
    dtc_scr[...] = _softplus(dtc_ref[...] + dbc_ref[...])
    a_c = -jnp.exp(alc_ref[...])
    a_r = -jnp.exp(alr_ref[...])

    t_idx = lax.broadcasted_iota(jnp.int32, (L, L), 0)
    s_idx = lax.broadcasted_iota(jnp.int32, (L, L), 1)
    lane_lo = lax.broadcasted_iota(jnp.int32, (L, LANES), 1) < P
    ex_row = lax.broadcasted_iota(jnp.int32, (2 * E, gw), 0)
    ex_head = jnp.right_shift(lax.broadcasted_iota(jnp.int32, (2 * E, gw), 1), P.bit_length() - 1)

    consts = []
    for d in range(2):
        incl = (s_idx <= t_idx) if d == 0 else (s_idx >= t_idx)
        incl_bf = jnp.where(incl, 1.0, 0.0).astype(BF16)
        incl_t_bf = jnp.where((t_idx <= s_idx) if d == 0 else (t_idx >= s_idx), 1.0, 0.0).astype(BF16)
        expand_bf = jnp.where(ex_row == ex_head + d * E, 1.0, 0.0).astype(BF16)
        consts.append((incl, incl_bf, incl_t_bf, expand_bf))
        if has_init:
            for e in range(E):
                h_scr[d, e * P:(e + 1) * P, :] = h0_ref[d, e]
        else:
            h_scr[d] = jnp.zeros((E * P, N), F32)

    def chunk(ci, carry):
        sd = []
        for d in range(2):
            incl, incl_bf, incl_t_bf, expand_bf = consts[d]
            last = L - 1 if d == 0 else 0
            c = ci if d == 0 else n_chunks - 1 - ci
            r0 = pl.multiple_of(c * L, L)
            s = dict(d=d, c=c, r0=r0, incl=incl, bc=bm_scr[pl.ds(r0, L), :], cc=cm_scr[pl.ds(r0, L), :],
                     h_all=h_scr[d])
            dt_r = _softplus(dtr_ref[c] + dbr_ref[...])
            da_c = dtc_scr[pl.ds(r0, L), :] * a_c
            da_r = dt_r * a_r
            s["cs_c"] = _dot_split(_split3(da_c), incl_bf, lhs_is_mask=True)
            cs_r = _dot_split(_split3(da_r), incl_t_bf, lhs_is_mask=False)
            s["cs_rd"] = cs_r[d * E:(d + 1) * E, :]
            s["dt_rd"] = dt_r[d * E:(d + 1) * E, :]
            s["tot_r"] = s["cs_rd"][:, last:last + 1]
            s["cb"] = lax.dot_general(s["cc"], s["bc"], (((1,), (1,)), ((), ())), preferred_element_type=F32)
            s["y_off"] = lax.dot_general(s["cc"], s["h_all"].astype(BF16), (((1,), (1,)), ((), ())),
                                         preferred_element_type=F32)
            s["expand_bf"] = expand_bf
            sd.append(s)
        for s in sd:
            s["ecs"] = _dot_split(_split2(jnp.exp(s["cs_c"])), s["expand_bf"], lhs_is_mask=False)
        for s in sd:
            d = s["d"]
            s["m"] = []
            for e in range(E):
                seg = s["cs_c"][:, d * E + e:d * E + e + 1] - s["cs_rd"][e:e + 1, :]
                m_e = s["cb"] * jnp.exp(jnp.where(s["incl"], seg, -jnp.inf)) * s["dt_rd"][e:e + 1, :]
                s["m"].append(m_e.astype(BF16))
        for s in sd:
            ys = []
            for j in range(n_pairs):
                xp = xsb_scr[pl.ds(s["r0"], L), j * LANES:(j + 1) * LANES]
                outs = [jnp.dot(s["m"][j * (LANES // P) + q], xp, preferred_element_type=F32)
                        for q in range(LANES // P)]
                ys.append(jnp.where(lane_lo, outs[0], outs[1]))
            y_scr = yf_scr if s["d"] == 0 else yb_scr
            y_scr[pl.ds(s["r0"], L), :] = jnp.concatenate(ys, axis=1) + s["ecs"] * s["y_off"]
        for s in sd:
            d, tot_r, h_all = s["d"], s["tot_r"], s["h_all"]
            wr = jnp.exp(tot_r - s["cs_rd"]) * s["dt_rd"]
            wr_big = jnp.concatenate([jnp.broadcast_to(wr[e:e + 1, :], (P, L)) for e in range(E)], axis=0)
            upd = jnp.dot((xst_scr[s["c"]] * wr_big).astype(BF16), s["bc"], preferred_element_type=F32)
            etot = jnp.exp(tot_r)
            for e in range(E):
                h_scr[d, e * P:(e + 1) * P, :] = (etot[e:e + 1, :] * h_all[e * P:(e + 1) * P, :]
                                                  + upd[e * P:(e + 1) * P, :])
        return carry

    lax.fori_loop(0, n_chunks, chunk, 0)
    if has_state_out:
        for d in range(2):
            for e in range(E):
                ho_ref[d, e] = h_scr[d, e * P:(e + 1) * P, :]

    y = yf_scr[...] + yb_scr[...] + dsk_ref[...] * xs_scr[...]
    yz = y * _silu(z_ref[...].astype(F32))
    ssq = jnp.sum(yz * yz, axis=1, keepdims=True)

    @pl.when(pl.program_id(1) == 0)
    def _():
        ssq_ref[...] = ssq

    @pl.when(pl.program_id(1) != 0)
    def _():
        ssq_ref[...] += ssq

    y_ref[...] = (yz * ng_ref[...]).astype(BF16)


def _ssd_core(zxbc, dt_raw, conv_w, conv_b, dt_bias, a_log, d_skip, norm_g, *,
              batch, seq, row_blk0, init=None, state_out=False, prev=None):
    m = zxbc.shape[0]
    G, P, N, L = SSD_GROUPS, SSD_HD, SSD_N, SSD_CHUNK
    di = d_skip.shape[0] * P
    heads = di // P
    E = heads // G
    gw = E * P
    nc = seq // L
    r0 = row_blk0 * seq
    dt4 = dt_raw[r0:r0 + batch * seq].reshape(batch, seq, 2, G, E)
    dtc = jnp.transpose(dt4, (0, 3, 1, 2, 4)).reshape(batch, G, seq, 2 * E)
    dtr = jnp.transpose(dt4.reshape(batch, nc, L, 2, G, E), (0, 4, 1, 3, 5, 2)).reshape(batch, G, nc, 2 * E, L)

    def per_group(v):
        v3 = jnp.transpose(v.reshape(2, G, E), (1, 0, 2)).reshape(G, 2 * E)
        return v3[:, None, :], v3[:, :, None]

    dbc, dbr = per_group(dt_bias)
    alc, alr = per_group(a_log)
    x_blk0 = di // gw
    b_blk0 = (2 * di) // N
    c_blk0 = (2 * di + G * N) // N
    in_specs = [
        pl.BlockSpec((seq, gw), lambda b, g: (row_blk0 + b, g)),
        pl.BlockSpec((seq, gw), lambda b, g: (row_blk0 + b, x_blk0 + g)),
        pl.BlockSpec((seq, N), lambda b, g: (row_blk0 + b, b_blk0 + g)),
        pl.BlockSpec((seq, N), lambda b, g: (row_blk0 + b, c_blk0 + g)),
        pl.BlockSpec((SSD_CONV_W, gw), lambda b, g: (0, g)),
        pl.BlockSpec((SSD_CONV_W, N), lambda b, g: (0, di // N + g)),
        pl.BlockSpec((SSD_CONV_W, N), lambda b, g: (0, di // N + G + g)),
        pl.BlockSpec((1, gw), lambda b, g: (0, g)),
        pl.BlockSpec((1, N), lambda b, g: (0, di // N + g)),
        pl.BlockSpec((1, N), lambda b, g: (0, di // N + G + g)),
        pl.BlockSpec((None, None, seq, 2 * E), lambda b, g: (b, g, 0, 0)),
        pl.BlockSpec((None, None, nc, 2 * E, L), lambda b, g: (b, g, 0, 0, 0)),
        pl.BlockSpec((None, 1, 2 * E), lambda b, g: (g, 0, 0)),
        pl.BlockSpec((None, 2 * E, 1), lambda b, g: (g, 0, 0)),
        pl.BlockSpec((None, 1, 2 * E), lambda b, g: (g, 0, 0)),
        pl.BlockSpec((None, 2 * E, 1), lambda b, g: (g, 0, 0)),
        pl.BlockSpec((1, gw), lambda b, g: (0, g)),
        pl.BlockSpec((1, gw), lambda b, g: (0, g)),
    ]
    cb2 = conv_b.reshape(1, -1)
    args = [zxbc, zxbc, zxbc, zxbc, conv_w, conv_w, conv_w, cb2, cb2, cb2,
            dtc, dtr, dbc, dbr, alc, alr,
            jnp.repeat(d_skip, P).reshape(1, di), norm_g.reshape(1, di)]
    if init is not None:
        in_specs.append(pl.BlockSpec((None, 2, E, P, N), lambda b, g: (b, 0, g, 0, 0)))
        args.append(init)
    out_specs = [
        pl.BlockSpec((seq, gw), lambda b, g: (row_blk0 + b, g)),
        pl.BlockSpec((seq, 1), lambda b, g: (row_blk0 + b, 0)),
    ]
    out_shape = [jax.ShapeDtypeStruct((m, di), BF16), jax.ShapeDtypeStruct((m, 1), F32)]
    if state_out:
        out_specs.append(pl.BlockSpec((None, 2, E, P, N), lambda b, g: (b, 0, g, 0, 0)))
        out_shape.append(jax.ShapeDtypeStruct((batch, 2, heads, P, N), F32))
    return _call_aliased(
        functools.partial(_ssd_kernel, n_chunks=nc, has_init=init is not None, has_state_out=state_out),
        {} if prev is None else {0: prev[0], 1: prev[1]},
        grid=(batch, G),
        in_specs=in_specs,
        args=args,
        out_specs=out_specs,
        out_shape=out_shape,
        scratch_shapes=[
            pltpu.VMEM((seq + 2 * CONV_PAD_ROWS, gw + 2 * N), BF16),
            pltpu.VMEM((seq, gw), F32),
            pltpu.VMEM((seq, gw), BF16),
            pltpu.VMEM((nc, gw, L), F32),
            pltpu.VMEM((seq, N), BF16),
            pltpu.VMEM((seq, N), BF16),
            pltpu.VMEM((seq, 2 * E), F32),
            pltpu.VMEM((seq, gw), F32),
            pltpu.VMEM((seq, gw), F32),
            pltpu.VMEM((2, E * P, N), F32),
        ],
        compiler_params=_params("parallel", "arbitrary"),
        name="ssd_core",
    )


def kernel(x_prompt, x_sample, cache_attn_k, cache_attn_v, state_mlstm_C, state_mlstm_n, state_mlstm_m, state_ssd_h, c, c_ctx, mod_w, mod_b, ln_g, ln_b, ffn_w_in, ffn_w_out, mlstm_w_in, mlstm_b_gate, mlstm_norm_g, mlstm_w_out, attn_w_qkv, attn_q_norm, attn_k_norm, attn_w_out, ssd_w_in, ssd_conv_w, ssd_conv_b, ssd_dt_bias, ssd_A_log, ssd_D, ssd_norm_g, ssd_w_out):
    bp, sp, d = x_prompt.shape
    bd, sd, _ = x_sample.shape
    depth = mod_w.shape[0]
    n_ctx = bp * sp
    m = n_ctx + bd * sd
    rows = _Rows(n_ctx, sd, d)
    alpha = (2 * depth) ** 0.25
    ctx_blk0, dec_blk0 = 0, n_ctx // sd

    cvec = jnp.zeros((N_SEG_PAD, d), F32).at[0].set(c_ctx).at[1:1 + bd].set(c)
    modv = _adaln(cvec, mod_w, mod_b).reshape(depth * N_SEG_PAD * N_MOD, 1, d)
    lnv = jnp.stack([ln_g, ln_b], axis=2).reshape(depth * 3 * 2, 1, d)

    x = jnp.concatenate([x_prompt.reshape(n_ctx, d), x_sample.reshape(bd * sd, d)], axis=0)
    h = _modulate(x, modv, rows, 0, 0, 1)

    qk_w, v_w = 2 * ML_HEADS * ML_DK, ML_HEADS * ML_DV
    mlstm_wt = jnp.swapaxes(mlstm_w_in, 1, 2)

    ml_states, attn_kv, ssd_states = None, [], []
    for i in range(depth):
        kind, j = i % 3, i // 3
        act = _ffn_in(h, ffn_w_in, (i, 0))
        x, h = _out_ln(act, ffn_w_out, (i, 0), x, modv, lnv, rows, layer=i, k_gate=2, ln_idx=3 * i,
                       alpha=alpha, coef=0.5, nxt=(i, 3, 4), name="ffn_out")
        ssq = None
        if kind == 0:
            qkv = _proj(h, mlstm_wt, (j,), 0, qk_w + v_w, BF16, scale_first=ML_DK ** -0.5,
                        bn=ML_HEADS * ML_DK, w_t=True, name="mlstm_qkv")
            og = _proj(h, mlstm_wt, (j,), qk_w + v_w, v_w, BF16, w_t=True, name="mlstm_o")
            n_gate = 4 * ML_HEADS
            gates = _small_proj(h, mlstm_wt, (j,), (qk_w + 2 * v_w) // n_gate, n_gate, w_t=True, name="mlstm_gates")
            y, *ml_states = _mlstm_core(qkv, og, gates, mlstm_b_gate[j], mlstm_norm_g[j],
                                        batch=bp, seq=sp, row_blk0=ctx_blk0,
                                        state_out=(mlstm_w_in.shape[0], j, ml_states))
            (y,) = _mlstm_core(qkv, og, gates, mlstm_b_gate[j], mlstm_norm_g[j],
                               batch=bd, seq=sd, row_blk0=dec_blk0, y_prev=y,
                               init=(state_mlstm_C, state_mlstm_n, state_mlstm_m, j))
            w_out = mlstm_w_out
        elif kind == 1:
            qkv = _proj(h, attn_w_qkv, (j,), 0, attn_w_qkv.shape[2], BF16, name="attn_qkv")
            y, kc, vc = _attn_core(qkv, attn_q_norm[j], attn_k_norm[j], batch=bp, seq=sp, row_blk0=ctx_blk0)
            past = cache_attn_k.shape[2]
            (y,) = _attn_core(qkv, attn_q_norm[j], attn_k_norm[j], batch=bd, seq=sd, row_blk0=dec_blk0,
                              rope=_rope_tables(sd), y_prev=y,
                              ctx=(cache_attn_k[:, j].reshape(bd, past, -1), cache_attn_v[:, j].reshape(bd, past, -1)))
            attn_kv.append((kc.reshape(bp, sp, ATTN_KV_HEADS, ATTN_HD), vc.reshape(bp, sp, ATTN_KV_HEADS, ATTN_HD)))
            w_out = attn_w_out
        else:
            di = ssd_D.shape[1] * SSD_HD
            zxbc_w = 2 * di + 2 * SSD_GROUPS * SSD_N
            zxbc = _proj(h, ssd_w_in, (j,), 0, zxbc_w, BF16, name="ssd_zxbc")
            dt_raw = _small_proj(h, ssd_w_in, (j,), zxbc_w // LANES, LANES, name="ssd_dt")
            y, ssq, hst = _ssd_core(zxbc, dt_raw, ssd_conv_w[j], ssd_conv_b[j], ssd_dt_bias[j], ssd_A_log[j],
                                    ssd_D[j], ssd_norm_g[j], batch=bp, seq=sp, row_blk0=ctx_blk0, state_out=True)
            y, ssq = _ssd_core(zxbc, dt_raw, ssd_conv_w[j], ssd_conv_b[j], ssd_dt_bias[j], ssd_A_log[j],
                               ssd_D[j], ssd_norm_g[j], batch=bd, seq=sd, row_blk0=dec_blk0,
                               init=state_ssd_h[:, j], prev=(y, ssq))
            ssd_states.append(hst)
            w_out = ssd_w_out
        x, h = _out_ln(y, w_out, (j,), x, modv, lnv, rows, layer=i, k_gate=5, ln_idx=3 * i + 1,
                       alpha=alpha, coef=1.0, nxt=(i, 6, 7), ssq=ssq,
                       ssq_dim=None if ssq is None else w_out.shape[1], bk=1024, x_dma=True, name="mix_out")
        act = _ffn_in(h, ffn_w_in, (i, 1))
        nxt = (i + 1, 0, 1) if i + 1 < depth else None
        x, h = _out_ln(act, ffn_w_out, (i, 1), x, modv, lnv, rows, layer=i, k_gate=8, ln_idx=3 * i + 2,
                       alpha=alpha, coef=0.5, nxt=nxt, name="ffn_out")

    y_prompt = x[:n_ctx].reshape(bp, sp, d)
    y_sample = x[n_ctx:].reshape(bd, sd, d)
    new_k = jnp.stack([kv[0] for kv in attn_kv], axis=1)
    new_v = jnp.stack([kv[1] for kv in attn_kv], axis=1)
    new_c = ml_states[0]
    new_n = ml_states[1].reshape(ml_states[1].shape[:4] + (ML_DK,))
    new_m = ml_states[2].reshape(ml_states[2].shape[:4])
    new_h = jnp.stack(ssd_states, axis=1)
    return (y_prompt, y_sample, new_k, new_v, new_c, new_n, new_m, new_h)
```
